```python
import math
import jax
import jax.numpy as jnp
from jax import lax
import numpy as np

D_MODEL = 1024
BATCH = 8
SEQ = 2048
DEPTH = 2
DEC_BATCH = 32
DEC_SEQ = 1
PAST_LEN = 16384
PAGE_SIZE = 128

RG_WIDTH = D_MODEL // 2
RG_HEADS = 8
RG_HEAD_DIM = RG_WIDTH // RG_HEADS
RG_C = 8.0
GDN_HEADS = 4
GDN_DK = 128
GDN_DV = 128
GDN_KEY_DIM = GDN_HEADS * GDN_DK
GDN_VAL_DIM = GDN_HEADS * GDN_DV
GDN_CHUNK = 64
CONV_W = 4
CONV_CH = RG_WIDTH + 2 * GDN_KEY_DIM + GDN_VAL_DIM
IN0_DIM = CONV_CH + RG_WIDTH + GDN_VAL_DIM + 2 * GDN_HEADS
MIX0_OUT = RG_WIDTH + GDN_VAL_DIM
ATT_HEADS = 8
ATT_HEAD_DIM = D_MODEL // ATT_HEADS
MOBA_BLOCK = 256
MOBA_TOPK = 3
MOBA_Q_BLOCK = 16
MOE_GROUPS = 4
MOE_EXPERTS_PER_GROUP = 4
MOE_EXPERTS = MOE_GROUPS * MOE_EXPERTS_PER_GROUP
MOE_TOPK = 2
MOE_FF = 512
RMS_EPS = 1e-6
L2_EPS = 1e-6
NEG_INF = -1e30

kernel_name = 'hybrid_rglru_gdn_moba_hmoe_step'


def _rmsnorm(x, g):
    xf = x.astype(jnp.float32)
    y = xf * lax.rsqrt(jnp.mean(xf * xf, axis=-1, keepdims=True) + RMS_EPS)
    return (y * g.astype(jnp.float32)).astype(x.dtype)


def _l2norm(x):
    xf = x.astype(jnp.float32)
    return xf * lax.rsqrt(jnp.sum(xf * xf, axis=-1, keepdims=True) + L2_EPS)


def _causal_dwconv(x, buf, w, b):
    t = x.shape[1]
    xp = jnp.concatenate([buf.astype(x.dtype), x], axis=1)
    y = b + xp[:, 0:t] * w[0]
    for j in range(1, CONV_W):
        y = y + xp[:, j:j + t] * w[j]
    return y, xp[:, t:]


def _rg_lru(x, h0, w_a, b_a, w_i, b_i, lam):
    bsz, t, _ = x.shape
    xg = x.reshape(bsz, t, RG_HEADS, RG_HEAD_DIM)
    r = jax.nn.sigmoid(jnp.einsum('bthi,hij->bthj', xg, w_a).reshape(bsz, t, RG_WIDTH) + b_a)
    gi = jax.nn.sigmoid(jnp.einsum('bthi,hij->bthj', xg, w_i).reshape(bsz, t, RG_WIDTH) + b_i)
    log_a = -RG_C * r.astype(jnp.float32) * jax.nn.softplus(-lam.astype(jnp.float32))
    a = jnp.exp(log_a)
    u = jnp.sqrt(-jnp.expm1(2.0 * log_a)) * (gi * x).astype(jnp.float32)
    u = u.at[:, 0].add(a[:, 0] * h0.astype(jnp.float32))

    def combine(left, right):
        a_l, u_l = left
        a_r, u_r = right
        return a_l * a_r, a_r * u_l + u_r

    _, h = lax.associative_scan(combine, (a, u), axis=1)
    return h.astype(x.dtype), h[:, -1].astype(h0.dtype)


def _gated_delta_chunked(q, k, v, g, beta, s0):
    bsz, t, nh, dk = k.shape
    dv = v.shape[-1]
    c = min(GDN_CHUNK, t)
    pad = (-t) % c
    n = (t + pad) // c

    def prep(arr):
        arr = jnp.pad(arr, [(0, 0), (0, pad)] + [(0, 0)] * (arr.ndim - 2))
        arr = arr.reshape((bsz, n, c) + arr.shape[2:])
        return jnp.moveaxis(arr, 3, 1)

    qc, kc, vc, gc, bc = prep(q), prep(k), prep(v), prep(g), prep(beta)
    gcum = jnp.cumsum(gc, axis=-1)
    idx = jnp.arange(c)
    lower = idx[:, None] >= idx[None, :]
    strict = idx[:, None] > idx[None, :]
    diff = gcum[..., :, None] - gcum[..., None, :]
    decay = jnp.where(lower, jnp.exp(jnp.where(lower, diff, 0.0)), 0.0)
    kb = kc * bc[..., None]
    vb = vc * bc[..., None]
    lmat = jnp.where(strict, jnp.einsum('bhncd,bhnsd->bhncs', kb, kc) * decay, 0.0)
    amat = lmat + jnp.eye(c, dtype=jnp.float32)
    rhs = jnp.concatenate([vb, kb * jnp.exp(gcum)[..., None]], axis=-1)
    sol = lax.linalg.triangular_solve(amat, rhs, left_side=True, lower=True, unit_diagonal=True)
    u, w = sol[..., :dv], sol[..., dv:]
    qk = jnp.where(lower, jnp.einsum('bhncd,bhnsd->bhncs', qc, kc) * decay, 0.0)

    def step(s, xs):
        q_i, k_i, u_i, w_i, qk_i, g_i = xs
        v_new = u_i - jnp.einsum('bhcd,bhde->bhce', w_i, s)
        o_i = (jnp.einsum('bhcd,bhde->bhce', q_i * jnp.exp(g_i)[..., None], s)
               + jnp.einsum('bhcs,bhse->bhce', qk_i, v_new))
        g_last = g_i[..., -1]
        s = (s * jnp.exp(g_last)[..., None, None]
             + jnp.einsum('bhcd,bhce->bhde', k_i * jnp.exp(g_last[..., None] - g_i)[..., None], v_new))
        return s, o_i

    xs = tuple(jnp.moveaxis(arr, 2, 0) for arr in (qc, kc, u, w, qk, gcum))
    s_final, o = lax.scan(step, s0, xs)
    o = jnp.moveaxis(o, 0, 2).reshape(bsz, nh, n * c, dv)[:, :, :t]
    return jnp.swapaxes(o, 1, 2), s_final


def _mixer_rglru_gdn(h, conv_buf, rg_h0, gdn_s0, w_in0, conv0_w, conv0_b, rg_wa, rg_ba, rg_wi, rg_bi,
                     rg_lambda, gdn_a_log, gdn_dt_bias, gdn_norm, w_out0):
    bsz, t, _ = h.shape
    proj = h @ w_in0
    conv_in = proj[..., :CONV_CH]
    rg_gate = proj[..., CONV_CH:CONV_CH + RG_WIDTH]
    z = proj[..., CONV_CH + RG_WIDTH:CONV_CH + RG_WIDTH + GDN_VAL_DIM]
    beta_raw = proj[..., IN0_DIM - 2 * GDN_HEADS:IN0_DIM - GDN_HEADS]
    alpha_raw = proj[..., IN0_DIM - GDN_HEADS:]
    conv_out, new_buf = _causal_dwconv(conv_in, conv_buf, conv0_w, conv0_b)
    rg_y, rg_hT = _rg_lru(conv_out[..., :RG_WIDTH], rg_h0, rg_wa, rg_ba, rg_wi, rg_bi, rg_lambda)
    rg_out = rg_y * jax.nn.gelu(rg_gate)
    qkv = jax.nn.silu(conv_out[..., RG_WIDTH:])
    q = _l2norm(qkv[..., :GDN_KEY_DIM].reshape(bsz, t, GDN_HEADS, GDN_DK)) * (GDN_DK ** -0.5)
    k = _l2norm(qkv[..., GDN_KEY_DIM:2 * GDN_KEY_DIM].reshape(bsz, t, GDN_HEADS, GDN_DK))
    v = qkv[..., 2 * GDN_KEY_DIM:].reshape(bsz, t, GDN_HEADS, GDN_DV).astype(jnp.float32)
    beta = jax.nn.sigmoid(beta_raw.astype(jnp.float32))
    g = -jnp.exp(gdn_a_log.astype(jnp.float32)) * jax.nn.softplus(
        alpha_raw.astype(jnp.float32) + gdn_dt_bias.astype(jnp.float32))
    o, s_T = _gated_delta_chunked(q, k, v, g, beta, gdn_s0.astype(jnp.float32))
    o = _rmsnorm(o, gdn_norm) * jax.nn.silu(z.reshape(bsz, t, GDN_HEADS, GDN_DV).astype(jnp.float32))
    gdn_out = o.reshape(bsz, t, GDN_VAL_DIM).astype(h.dtype)
    out = jnp.concatenate([rg_out, gdn_out], axis=-1) @ w_out0
    return out, new_buf, rg_hT, s_T.astype(gdn_s0.dtype)


def _moba_attend(q, k_own, v_own, own_mask, k_sel=None, v_sel=None):
    scale = ATT_HEAD_DIM ** -0.5
    s_own = jnp.einsum('bhqd,bhsd->bhqs', q, k_own).astype(jnp.float32) * scale
    s_own = jnp.where(own_mask, s_own, NEG_INF)
    if k_sel is None:
        p = jax.nn.softmax(s_own, axis=-1).astype(v_own.dtype)
        return jnp.einsum('bhqs,bhsd->bhqd', p, v_own)
    b, h, nq = q.shape[:3]
    n_sel = k_sel.shape[3]
    n_rows = n_sel * MOBA_BLOCK
    s_sel = jnp.einsum('bhqd,bhqnsd->bhqns', q, k_sel).astype(jnp.float32) * scale
    s = jnp.concatenate([s_sel.reshape(b, h, nq, n_rows), s_own], axis=-1)
    p = jax.nn.softmax(s, axis=-1).astype(v_own.dtype)
    p_sel = p[..., :n_rows].reshape(b, h, nq, n_sel, MOBA_BLOCK)
    return (jnp.einsum('bhqns,bhqnsd->bhqd', p_sel, v_sel)
            + jnp.einsum('bhqs,bhsd->bhqd', p[..., n_rows:], v_own))


def _moba_prompt(q, k, v):
    bsz, t, nh, hd = q.shape
    qh, kh, vh = (jnp.swapaxes(arr, 1, 2) for arr in (q, k, v))
    n_blk = -(-t // MOBA_BLOCK)
    n_full = t // MOBA_BLOCK
    kb = kh[:, :, :n_full * MOBA_BLOCK].reshape(bsz, nh, n_full, MOBA_BLOCK, hd)
    vb = vh[:, :, :n_full * MOBA_BLOCK].reshape(bsz, nh, n_full, MOBA_BLOCK, hd)
    kmean = jnp.mean(kb.astype(jnp.float32), axis=3).astype(q.dtype)
    bi = jnp.arange(bsz)[:, None, None, None]
    hi = jnp.arange(nh)[None, :, None, None]
    outs = []
    for blk in range(n_blk):
        start = blk * MOBA_BLOCK
        length = min(MOBA_BLOCK, t - start)
        k_own = kh[:, :, start:start + length]
        v_own = vh[:, :, start:start + length]
        n_sub = length // MOBA_Q_BLOCK
        q_sub_all = jnp.moveaxis(
            qh[:, :, start:start + length].reshape(bsz, nh, n_sub, MOBA_Q_BLOCK, hd), 2, 0)
        n_sel = min(MOBA_TOPK, blk)
        kpos = jnp.arange(length)

        def one(args, blk=blk, n_sel=n_sel, k_own=k_own, v_own=v_own, kpos=kpos):
            q_sub, off = args
            qpos = off + jnp.arange(MOBA_Q_BLOCK)
            mask = kpos[None, :] <= qpos[:, None]
            if n_sel == 0:
                return _moba_attend(q_sub, k_own, v_own, mask)
            gate = jnp.einsum('bhqd,bhnd->bhqn', q_sub, kmean[:, :, :blk]).astype(jnp.float32)
            _, idx = lax.top_k(gate, n_sel)
            return _moba_attend(q_sub, k_own, v_own, mask, kb[bi, hi, idx], vb[bi, hi, idx])

        o = lax.map(one, (q_sub_all, jnp.arange(n_sub) * MOBA_Q_BLOCK))
        outs.append(jnp.moveaxis(o, 0, 2).reshape(bsz, nh, length, hd))
    return jnp.swapaxes(jnp.concatenate(outs, axis=2), 1, 2)


def _moba_sample(q, k_new, v_new, cache_k, cache_v, page_table):
    bsz, t, nh, hd = q.shape
    n_pages = page_table.shape[1]
    ppb = MOBA_BLOCK // PAGE_SIZE
    n_full = n_pages // ppb
    own_pages = n_pages - n_full * ppb
    own_past = own_pages * PAGE_SIZE
    qh = jnp.swapaxes(q, 1, 2)
    k_own, v_own = k_new, v_new
    if own_pages:
        own_phys = page_table[:, n_full * ppb:]
        k_own = jnp.concatenate(
            [cache_k[own_phys].reshape(bsz, own_past, nh, hd).astype(k_new.dtype), k_new], axis=1)
        v_own = jnp.concatenate(
            [cache_v[own_phys].reshape(bsz, own_past, nh, hd).astype(v_new.dtype), v_new], axis=1)
    k_own = jnp.swapaxes(k_own, 1, 2)
    v_own = jnp.swapaxes(v_own, 1, 2)
    jj = jnp.arange(own_past + t)
    ii = jnp.arange(t)
    mask = (jj[None, :] < own_past) | (jj[None, :] - own_past <= ii[:, None])
    if n_full == 0:
        return jnp.swapaxes(_moba_attend(qh, k_own, v_own, mask), 1, 2)
    past_rows = cache_k[page_table[:, :n_full * ppb]]
    blk_sum = jnp.sum(past_rows.reshape(bsz, n_full, MOBA_BLOCK, nh, hd), axis=2, dtype=jnp.float32)
    kmean = (blk_sum / MOBA_BLOCK).astype(q.dtype)
    gate = jnp.einsum('bhqd,bnhd->bhqn', qh, kmean).astype(jnp.float32)
    n_sel = min(MOBA_TOPK, n_full)
    _, idx = lax.top_k(gate, n_sel)
    lpage = idx[..., None] * ppb + jnp.arange(ppb)
    phys = page_table[jnp.arange(bsz)[:, None, None, None, None], lpage]
    hsel = jnp.arange(nh)[None, :, None, None, None]
    k_sel = cache_k[phys, :, hsel].reshape(bsz, nh, t, n_sel, MOBA_BLOCK, hd).astype(q.dtype)
    v_sel = cache_v[phys, :, hsel].reshape(bsz, nh, t, n_sel, MOBA_BLOCK, hd).astype(q.dtype)
    return jnp.swapaxes(_moba_attend(qh, k_own, v_own, mask, k_sel, v_sel), 1, 2)


def _qkv_heads(h, w_qkv1):
    bsz, t, _ = h.shape
    qkv = (h @ w_qkv1).reshape(bsz, t, 3, ATT_HEADS, ATT_HEAD_DIM)
    return qkv[:, :, 0], qkv[:, :, 1], qkv[:, :, 2]


def _hier_moe(x, w_group, b_group, w_router, b_router, w_gate, w_up, w_down):
    n = x.shape[0]
    g_prob = jax.nn.softmax((x @ w_group + b_group).astype(jnp.float32), axis=-1)
    g_top, g_idx = lax.top_k(g_prob, 1)
    e_logits = (x @ w_router + b_router).astype(jnp.float32).reshape(n, MOE_GROUPS, MOE_EXPERTS_PER_GROUP)
    e_in = e_logits[jnp.arange(n), g_idx[:, 0]]
    e_top, e_idx = lax.top_k(e_in, MOE_TOPK)
    wts = jax.nn.softmax(e_top, axis=-1) * g_top
    expert = g_idx * MOE_EXPERTS_PER_GROUP + e_idx
    gates = jnp.sum(jax.nn.one_hot(expert, MOE_EXPERTS, dtype=jnp.float32) * wts[..., None], axis=1)
    hid = jax.nn.silu(jnp.einsum('nd,edf->nef', x, w_gate)) * jnp.einsum('nd,edf->nef', x, w_up)
    hid = hid * gates.astype(x.dtype)[..., None]
    return jnp.einsum('nef,efd->nd', hid, w_down)


def _normal(k, shape, scale):
    return jax.random.normal(k, shape, jnp.float32) * scale


def setup_inputs(seed: int = 0) -> dict:
    key = jax.random.key(seed)
    ks = jax.random.split(key, 32)
    n_pages = PAST_LEN // PAGE_SIZE
    n_used = DEC_BATCH * n_pages
    n_pool = n_used + (n_used + 3) // 4
    perm = jax.random.permutation(ks[7], n_pool)
    page_table = perm[:n_used].reshape(DEC_BATCH, n_pages).astype(jnp.int32)
    u = jax.random.uniform(ks[18], (RG_WIDTH,), jnp.float32, 0.9, 0.999) ** (1.0 / RG_C)
    rg_lambda = jnp.log(u) - jnp.log1p(-u)
    gdn_a_log = jnp.log(jax.random.uniform(ks[19], (GDN_HEADS,), jnp.float32, 1.0, 16.0))
    dt = jnp.exp(jax.random.uniform(ks[20], (GDN_HEADS,), jnp.float32, math.log(1e-3), math.log(1e-1)))
    gdn_dt_bias = dt + jnp.log(-jnp.expm1(-dt))
    return {
        'x_prompt': _normal(ks[0], (BATCH, SEQ, D_MODEL), 1.0),
        'x_sample': _normal(ks[1], (DEC_BATCH, DEC_SEQ, D_MODEL), 1.0),
        'state_conv': _normal(ks[2], (DEC_BATCH, CONV_W - 1, CONV_CH), 1.0),
        'state_rglru_h': _normal(ks[3], (DEC_BATCH, RG_WIDTH), 0.5),
        'state_gdn': _normal(ks[4], (DEC_BATCH, GDN_HEADS, GDN_DK, GDN_DV), 0.1),
        'cache_k': _normal(ks[5], (n_pool, PAGE_SIZE, ATT_HEADS, ATT_HEAD_DIM), 1.0),
        'cache_v': _normal(ks[6], (n_pool, PAGE_SIZE, ATT_HEADS, ATT_HEAD_DIM), 1.0),
        'page_table': page_table,
        'norm_mix': 1.0 + _normal(ks[8], (DEPTH, D_MODEL), 0.01),
        'norm_ffn': 1.0 + _normal(ks[9], (DEPTH, D_MODEL), 0.01),
        'norm_final': 1.0 + _normal(ks[10], (D_MODEL,), 0.01),
        'w_in0': _normal(ks[11], (D_MODEL, IN0_DIM), D_MODEL ** -0.5),
        'conv0_w': _normal(ks[12], (CONV_W, CONV_CH), CONV_W ** -0.5),
        'conv0_b': _normal(ks[13], (CONV_CH,), 0.01),
        'rg_wa': _normal(ks[14], (RG_HEADS, RG_HEAD_DIM, RG_HEAD_DIM), RG_HEAD_DIM ** -0.5),
        'rg_ba': _normal(ks[15], (RG_WIDTH,), 0.01),
        'rg_wi': _normal(ks[16], (RG_HEADS, RG_HEAD_DIM, RG_HEAD_DIM), RG_HEAD_DIM ** -0.5),
        'rg_bi': _normal(ks[17], (RG_WIDTH,), 0.01),
        'rg_lambda': rg_lambda,
        'gdn_a_log': gdn_a_log,
        'gdn_dt_bias': gdn_dt_bias,
        'gdn_norm': 1.0 + _normal(ks[21], (GDN_DV,), 0.01),
        'w_out0': _normal(ks[22], (MIX0_OUT, D_MODEL), MIX0_OUT ** -0.5),
        'w_qkv1': _normal(ks[23], (D_MODEL, 3 * ATT_HEADS * ATT_HEAD_DIM), D_MODEL ** -0.5),
        'w_out1': _normal(ks[24], (ATT_HEADS * ATT_HEAD_DIM, D_MODEL), (ATT_HEADS * ATT_HEAD_DIM) ** -0.5),
        'moe_w_group': _normal(ks[25], (DEPTH, D_MODEL, MOE_GROUPS), D_MODEL ** -0.5),
        'moe_b_group': _normal(ks[26], (DEPTH, MOE_GROUPS), 0.01),
        'moe_w_router': _normal(ks[27], (DEPTH, D_MODEL, MOE_EXPERTS), D_MODEL ** -0.5),
        'moe_b_router': _normal(ks[28], (DEPTH, MOE_EXPERTS), 0.01),
        'moe_w_gate': _normal(ks[29], (DEPTH, MOE_EXPERTS, D_MODEL, MOE_FF), D_MODEL ** -0.5),
        'moe_w_up': _normal(ks[30], (DEPTH, MOE_EXPERTS, D_MODEL, MOE_FF), D_MODEL ** -0.5),
        'moe_w_down': _normal(ks[31], (DEPTH, MOE_EXPERTS, MOE_FF, D_MODEL), MOE_FF ** -0.5),
    }


def reference(x_prompt, x_sample, state_conv, state_rglru_h, state_gdn, cache_k, cache_v, page_table,
              norm_mix, norm_ffn, norm_final, w_in0, conv0_w, conv0_b, rg_wa, rg_ba, rg_wi, rg_bi,
              rg_lambda, gdn_a_log, gdn_dt_bias, gdn_norm, w_out0, w_qkv1, w_out1,
              moe_w_group, moe_b_group, moe_w_router, moe_b_router, moe_w_gate, moe_w_up, moe_w_down):
    bp, tp, _ = x_prompt.shape
    bs, ts, _ = x_sample.shape
    xp, xs = x_prompt, x_sample
    for layer in range(DEPTH):
        hp = _rmsnorm(xp, norm_mix[layer])
        hs = _rmsnorm(xs, norm_mix[layer])
        if layer % 2 == 0:
            dt = xp.dtype
            mix_w = (w_in0, conv0_w, conv0_b, rg_wa, rg_ba, rg_wi, rg_bi, rg_lambda,
                     gdn_a_log, gdn_dt_bias, gdn_norm, w_out0)
            op, p_conv, p_h, p_gdn = _mixer_rglru_gdn(
                hp, jnp.zeros((bp, CONV_W - 1, CONV_CH), dt), jnp.zeros((bp, RG_WIDTH), dt),
                jnp.zeros((bp, GDN_HEADS, GDN_DK, GDN_DV), dt), *mix_w)
            os_, s_conv, s_h, s_gdn = _mixer_rglru_gdn(hs, state_conv, state_rglru_h, state_gdn, *mix_w)
        else:
            qp, p_k, p_v = _qkv_heads(hp, w_qkv1)
            op = _moba_prompt(qp, p_k, p_v).reshape(bp, tp, ATT_HEADS * ATT_HEAD_DIM) @ w_out1
            qs, s_k, s_v = _qkv_heads(hs, w_qkv1)
            os_ = _moba_sample(qs, s_k, s_v, cache_k, cache_v, page_table).reshape(
                bs, ts, ATT_HEADS * ATT_HEAD_DIM) @ w_out1
        xp = xp + op
        xs = xs + os_
        h_all = jnp.concatenate([_rmsnorm(xp, norm_ffn[layer]).reshape(-1, D_MODEL),
                                 _rmsnorm(xs, norm_ffn[layer]).reshape(-1, D_MODEL)], axis=0)
        f = _hier_moe(h_all, moe_w_group[layer], moe_b_group[layer], moe_w_router[layer],
                      moe_b_router[layer], moe_w_gate[layer], moe_w_up[layer], moe_w_down[layer])
        xp = xp + f[:bp * tp].reshape(xp.shape)
        xs = xs + f[bp * tp:].reshape(xs.shape)
    y_prompt = _rmsnorm(xp, norm_final)
    y_sample = _rmsnorm(xs, norm_final)
    return (y_prompt, y_sample, p_conv, p_h, p_gdn, p_k, p_v, s_conv, s_h, s_gdn, s_k, s_v)
```

```python
import functools

import jax
import jax.numpy as jnp
from jax import lax
from jax.experimental import pallas as pl
from jax.experimental.pallas import tpu as pltpu

F32 = jnp.float32
BF16 = jnp.bfloat16
HIGHEST = lax.Precision.HIGHEST

D_MODEL = 1024
RG_WIDTH = 512
RG_HEADS = 8
RG_HEAD_DIM = 64
RG_C = 8.0
GDN_HEADS = 4
GDN_DK = 128
GDN_DV = 128
GDN_CHUNK = 64
CONV_W = 4
CONV_CH = 2048
IN0_DIM = 3080
IN0_PAD = 3200
ATT_HEADS = 8
ATT_HEAD_DIM = 128
MOBA_BLOCK = 256
MOBA_TOPK = 3
PAGE_SIZE = 128
PAGES_PER_BLOCK = MOBA_BLOCK // PAGE_SIZE
MOE_GROUPS = 4
MOE_EXPERTS_PER_GROUP = 4
MOE_EXPERTS = 16
MOE_FF = 512
RMS_EPS = 1e-6
L2_EPS = 1e-6
NEG_INF = -1e30

LANES = 128
SUBLANES = 8
VMEM_LIMIT = 56 * 1024 * 1024

NT_DIMS = (((1,), (1,)), ((), ()))
TN_DIMS = (((0,), (0,)), ((), ()))


def _params(*sem):
    return pltpu.CompilerParams(dimension_semantics=sem, vmem_limit_bytes=VMEM_LIMIT)


def _rms(x, g):
    return x * lax.rsqrt(jnp.mean(x * x, axis=-1, keepdims=True) + RMS_EPS) * g


def _softplus(x):
    return jnp.maximum(x, 0.0) + jnp.log1p(jnp.exp(-jnp.abs(x)))


def _sigmoid(x):
    return 1.0 / (1.0 + jnp.exp(-x))


def _norm_matmul_kernel(x_ref, g_ref, w_ref, *out_refs, widths):
    h = _rms(x_ref[...], g_ref[...]).astype(BF16)
    lo = 0
    for o_ref, width in zip(out_refs, widths):
        o_ref[...] = jnp.dot(h, w_ref[:, lo:lo + width], preferred_element_type=F32)
        lo += width


def _norm_matmul(x, g, w_bf16, widths, tm):
    n, d = x.shape
    assert n % tm == 0 and sum(widths) == w_bf16.shape[1]
    return pl.pallas_call(
        functools.partial(_norm_matmul_kernel, widths=widths),
        grid=(n // tm,),
        in_specs=[pl.BlockSpec((tm, d), lambda i: (i, 0)),
                  pl.BlockSpec((1, d), lambda i: (0, 0)),
                  pl.BlockSpec(w_bf16.shape, lambda i: (0, 0))],
        out_specs=[pl.BlockSpec((tm, wd), lambda i: (i, 0)) for wd in widths],
        out_shape=[jax.ShapeDtypeStruct((n, wd), F32) for wd in widths],
        compiler_params=_params("parallel"),
        name="norm_matmul",
    )(x, g.reshape(1, d), w_bf16)


def _matmul_residual_kernel(a_ref, w_ref, r_ref, o_ref):
    o_ref[...] = r_ref[...] + jnp.dot(a_ref[...].astype(BF16), w_ref[...], preferred_element_type=F32)


def _matmul_residual(a, w_bf16, res, tm):
    n, k = a.shape
    m = w_bf16.shape[1]
    assert n % tm == 0
    return pl.pallas_call(
        _matmul_residual_kernel,
        grid=(n // tm,),
        in_specs=[pl.BlockSpec((tm, k), lambda i: (i, 0)),
                  pl.BlockSpec((k, m), lambda i: (0, 0)),
                  pl.BlockSpec((tm, m), lambda i: (i, 0))],
        out_specs=pl.BlockSpec((tm, m), lambda i: (i, 0)),
        out_shape=jax.ShapeDtypeStruct((n, m), F32),
        compiler_params=_params("parallel"),
        name="matmul_residual",
    )(a, w_bf16, res)


def _linear_scan(a, u, n_rows):
    row = lax.broadcasted_iota(jnp.int32, a.shape, 0)
    d = 1
    while d < n_rows:
        a_prev = jnp.where(row >= d, pltpu.roll(a, d, 0), 1.0)
        u_prev = jnp.where(row >= d, pltpu.roll(u, d, 0), 0.0)
        u = a * u_prev + u
        a = a * a_prev
        d *= 2
    return a, u


def _unit_lower_inverse(lmat, c):
    row = lax.broadcasted_iota(jnp.int32, (c, c), 0)
    col = lax.broadcasted_iota(jnp.int32, (c, c), 1)
    eye = jnp.where(row == col, 1.0, 0.0)
    m = -lmat
    inv = eye + m
    p = 2
    while p < c:
        m = jnp.dot(m, m, precision=HIGHEST, preferred_element_type=F32)
        inv = inv + jnp.dot(inv, m, precision=HIGHEST, preferred_element_type=F32)
        p *= 2
    return inv


def _mixer_kernel(conv_ref, gate_ref, z_ref, ba_ref, x_ref, cbuf0_ref, h0_ref, s0_ref,
                  cw_ref, cb_ref, wa_ref, rba_ref, wi_ref, rbi_ref, lam_ref, alog_ref, dtb_ref, gnorm_ref, wout_ref,
                  x1_ref, ht_ref, st_ref,
                  cbuf, hcar, state, act, *, tc, chunk, t_valid, nt):
    t = pl.program_id(1)

    @pl.when(t == 0)
    def _():
        cbuf[0:SUBLANES, :] = cbuf0_ref[0]
        hcar[...] = h0_ref[0]
        state[...] = s0_ref[0]

    cbuf[SUBLANES:SUBLANES + tc, :] = conv_ref[...]
    base = SUBLANES - (CONV_W - 1)
    y = cb_ref[...] + cbuf[base:base + tc, :] * cw_ref[0:1, :]
    for j in range(1, CONV_W):
        y = y + cbuf[base + j:base + j + tc, :] * cw_ref[j:j + 1, :]
    cbuf[0:SUBLANES, :] = cbuf[tc:tc + SUBLANES, :]

    masked = t_valid < nt * tc
    if masked:
        valid = (t * tc + lax.broadcasted_iota(jnp.int32, (tc, 1), 0)) < t_valid

    xr = y[:, :RG_WIDTH]
    xr_b = xr.astype(BF16)
    pair = 2 * RG_HEAD_DIM
    r_lin = jnp.concatenate(
        [jnp.dot(xr_b[:, pair * i:pair * (i + 1)], wa_ref[i], preferred_element_type=F32) for i in range(RG_WIDTH // pair)], axis=-1)
    i_lin = jnp.concatenate(
        [jnp.dot(xr_b[:, pair * i:pair * (i + 1)], wi_ref[i], preferred_element_type=F32) for i in range(RG_WIDTH // pair)], axis=-1)
    r = _sigmoid(r_lin + rba_ref[...])
    gi = _sigmoid(i_lin + rbi_ref[...])
    log_a = -RG_C * r * _softplus(-lam_ref[...])
    a = jnp.exp(log_a)
    u = jnp.sqrt(-jnp.tanh(log_a) * (a * a + 1.0)) * (gi * xr)
    if masked:
        a = jnp.where(valid, a, 1.0)
        u = jnp.where(valid, u, 0.0)
    a_cum, h_loc = _linear_scan(a, u, tc)
    h = h_loc + a_cum * hcar[...]
    hcar[...] = h[tc - 1:tc, :]
    act[:, 0:RG_WIDTH] = (h * jax.nn.gelu(gate_ref[...])).astype(BF16)

    qkv = y[:, RG_WIDTH:]
    qkv = qkv * _sigmoid(qkv)
    ba = ba_ref[...]
    beta_all = _sigmoid(ba)
    g_all = -jnp.exp(alog_ref[...]) * _softplus(ba + dtb_ref[...])
    if masked:
        beta_all = jnp.where(valid, beta_all, 0.0)
        g_all = jnp.where(valid, g_all, 0.0)
    rr = lax.broadcasted_iota(jnp.int32, (tc, tc), 0)
    cc = lax.broadcasted_iota(jnp.int32, (tc, tc), 1)
    tri = jnp.where((cc <= rr) & (cc >= (rr // chunk) * chunk), 1.0, 0.0)
    gcum_all = jnp.dot(tri, g_all, precision=HIGHEST, preferred_element_type=F32)

    crow = lax.broadcasted_iota(jnp.int32, (chunk, chunk), 0)
    ccol = lax.broadcasted_iota(jnp.int32, (chunk, chunk), 1)
    lower = crow >= ccol
    strict = crow > ccol
    diag = crow == ccol
    gnorm = gnorm_ref[...]
    z_all = z_ref[...]

    for hd in range(GDN_HEADS):
        qh_all = qkv[:, GDN_DK * hd:GDN_DK * (hd + 1)]
        kh_all = qkv[:, 512 + GDN_DK * hd:512 + GDN_DK * (hd + 1)]
        vh_all = qkv[:, 1024 + GDN_DV * hd:1024 + GDN_DV * (hd + 1)]
        qh_all = qh_all * lax.rsqrt(jnp.sum(qh_all * qh_all, axis=-1, keepdims=True) + L2_EPS) * (GDN_DK ** -0.5)
        kh_all = kh_all * lax.rsqrt(jnp.sum(kh_all * kh_all, axis=-1, keepdims=True) + L2_EPS)
        s_h = state[hd]
        for ci in range(tc // chunk):
            lo = ci * chunk
            qh = qh_all[lo:lo + chunk]
            kh = kh_all[lo:lo + chunk]
            vh = vh_all[lo:lo + chunk]
            beta = beta_all[lo:lo + chunk, hd:hd + 1]
            gcol = gcum_all[lo:lo + chunk, GDN_HEADS + hd:GDN_HEADS + hd + 1]
            grow = jnp.sum(jnp.where(diag, gcol, 0.0), axis=0, keepdims=True)
            decay = jnp.where(lower, jnp.exp(jnp.where(lower, gcol - grow, 0.0)), 0.0)
            kb = kh * beta
            vb = vh * beta
            kh_b = kh.astype(BF16)
            kk = lax.dot_general(kb.astype(BF16), kh_b, NT_DIMS, preferred_element_type=F32)
            inv = _unit_lower_inverse(jnp.where(strict, kk * decay, 0.0), chunk)
            rhs = jnp.concatenate([vb, kb * jnp.exp(gcol)], axis=-1)
            sol = jnp.dot(inv, rhs, precision=HIGHEST, preferred_element_type=F32)
            u_c = sol[:, :GDN_DV]
            w_c = sol[:, GDN_DV:]
            qk = jnp.where(lower, lax.dot_general(qh.astype(BF16), kh_b, NT_DIMS, preferred_element_type=F32) * decay, 0.0)
            s_b = s_h.astype(BF16)
            v_new = u_c - jnp.dot(w_c.astype(BF16), s_b, preferred_element_type=F32)
            v_new_b = v_new.astype(BF16)
            o = (jnp.dot((qh * jnp.exp(gcol)).astype(BF16), s_b, preferred_element_type=F32)
                 + jnp.dot(qk.astype(BF16), v_new_b, preferred_element_type=F32))
            g_last = gcol[chunk - 1:chunk, :]
            k_dec = (kh * jnp.exp(g_last - gcol)).astype(BF16)
            s_h = s_h * jnp.exp(g_last) + lax.dot_general(k_dec, v_new_b, TN_DIMS, preferred_element_type=F32)
            zh = z_all[lo:lo + chunk, GDN_DV * hd:GDN_DV * (hd + 1)]
            o = _rms(o, gnorm) * (zh * _sigmoid(zh))
            act[lo:lo + chunk, RG_WIDTH + GDN_DV * hd:RG_WIDTH + GDN_DV * (hd + 1)] = o.astype(BF16)
        state[hd] = s_h

    x1_ref[...] = x_ref[...] + jnp.dot(act[...], wout_ref[...], preferred_element_type=F32)

    @pl.when(t == nt - 1)
    def _():
        ht_ref[0] = hcar[...]
        st_ref[0] = state[...]


def _mixer(proj, x, cbuf0, h0, s0, wts, *, n_seq, t_len, tc, chunk, t_valid):
    nt = t_len // tc
    assert t_len % tc == 0 and tc % chunk == 0 and tc % SUBLANES == 0
    row = lambda b, t: b * nt + t
    const2 = lambda b, t: (0, 0)
    (cw, cb, wa4, rba, wi4, rbi, lam, alog, dtb, gnorm, wout) = wts
    in_specs = [
        pl.BlockSpec((tc, CONV_CH), lambda b, t: (row(b, t), 0)),
        pl.BlockSpec((tc, RG_WIDTH), lambda b, t: (row(b, t), CONV_CH // RG_WIDTH)),
        pl.BlockSpec((tc, RG_WIDTH), lambda b, t: (row(b, t), CONV_CH // RG_WIDTH + 1)),
        pl.BlockSpec((tc, LANES), lambda b, t: (row(b, t), (CONV_CH + 2 * RG_WIDTH) // LANES)),
        pl.BlockSpec((tc, D_MODEL), lambda b, t: (row(b, t), 0)),
        pl.BlockSpec((1, SUBLANES, CONV_CH), lambda b, t: (b, 0, 0)),
        pl.BlockSpec((1, 1, RG_WIDTH), lambda b, t: (b, 0, 0)),
        pl.BlockSpec((1, GDN_HEADS, GDN_DK, GDN_DV), lambda b, t: (b, 0, 0, 0)),
        pl.BlockSpec(cw.shape, const2), pl.BlockSpec(cb.shape, const2),
        pl.BlockSpec(wa4.shape, lambda b, t: (0, 0, 0)), pl.BlockSpec(rba.shape, const2),
        pl.BlockSpec(wi4.shape, lambda b, t: (0, 0, 0)), pl.BlockSpec(rbi.shape, const2),
        pl.BlockSpec(lam.shape, const2), pl.BlockSpec(alog.shape, const2), pl.BlockSpec(dtb.shape, const2),
        pl.BlockSpec(gnorm.shape, const2), pl.BlockSpec(wout.shape, const2),
    ]
    out_specs = [
        pl.BlockSpec((tc, D_MODEL), lambda b, t: (row(b, t), 0)),
        pl.BlockSpec((1, 1, RG_WIDTH), lambda b, t: (b, 0, 0)),
        pl.BlockSpec((1, GDN_HEADS, GDN_DK, GDN_DV), lambda b, t: (b, 0, 0, 0)),
    ]
    out_shape = [
        jax.ShapeDtypeStruct((n_seq * t_len, D_MODEL), F32),
        jax.ShapeDtypeStruct((n_seq, 1, RG_WIDTH), F32),
        jax.ShapeDtypeStruct((n_seq, GDN_HEADS, GDN_DK, GDN_DV), F32),
    ]
    return pl.pallas_call(
        functools.partial(_mixer_kernel, tc=tc, chunk=chunk, t_valid=t_valid, nt=nt),
        grid=(n_seq, nt),
        in_specs=in_specs, out_specs=out_specs, out_shape=out_shape,
        scratch_shapes=[pltpu.VMEM((tc + SUBLANES, CONV_CH), F32),
                        pltpu.VMEM((1, RG_WIDTH), F32),
                        pltpu.VMEM((GDN_HEADS, GDN_DK, GDN_DV), F32),
                        pltpu.VMEM((tc, D_MODEL), BF16)],
        compiler_params=_params("arbitrary", "arbitrary"),
        name="mixer0",
    )(proj, proj, proj, proj, x, cbuf0, h0, s0, cw, cb, wa4, rba, wi4, rbi, lam, alog, dtb, gnorm, wout)


def _router_kernel(x_ref, g_ref, w_ref, b_ref, h_ref, gates_ref):
    h = _rms(x_ref[...], g_ref[...])
    h_ref[...] = h.astype(BF16)
    logits = jnp.dot(h, w_ref[...], precision=HIGHEST, preferred_element_type=F32) + b_ref[...]
    lane = lax.broadcasted_iota(jnp.int32, logits.shape, 1)
    big = jnp.int32(1 << 20)
    is_grp = (lane >= MOE_EXPERTS) & (lane < MOE_EXPERTS + MOE_GROUPS)
    gl = jnp.where(is_grp, logits, -jnp.inf)
    gmax = jnp.max(gl, axis=-1, keepdims=True)
    gidx = jnp.min(jnp.where(gl == gmax, lane, big), axis=-1, keepdims=True) - MOE_EXPERTS
    g_top = 1.0 / jnp.sum(jnp.exp(gl - gmax), axis=-1, keepdims=True)
    in_grp = (lane >= gidx * MOE_EXPERTS_PER_GROUP) & (lane < (gidx + 1) * MOE_EXPERTS_PER_GROUP)
    el = jnp.where(in_grp, logits, -jnp.inf)
    m1 = jnp.max(el, axis=-1, keepdims=True)
    i1 = jnp.min(jnp.where(el == m1, lane, big), axis=-1, keepdims=True)
    el2 = jnp.where(lane == i1, -jnp.inf, el)
    m2 = jnp.max(el2, axis=-1, keepdims=True)
    i2 = jnp.min(jnp.where(el2 == m2, lane, big), axis=-1, keepdims=True)
    e2 = jnp.exp(m2 - m1)
    w1 = g_top / (1.0 + e2)
    w2 = g_top * e2 / (1.0 + e2)
    gates_ref[...] = jnp.where(lane == i1, w1, 0.0) + jnp.where(lane == i2, w2, 0.0)


def _moe_router(x, g, w_cat, b_cat, tm):
    n, d = x.shape
    return pl.pallas_call(
        _router_kernel,
        grid=(n // tm,),
        in_specs=[pl.BlockSpec((tm, d), lambda i: (i, 0)),
                  pl.BlockSpec((1, d), lambda i: (0, 0)),
                  pl.BlockSpec((d, LANES), lambda i: (0, 0)),
                  pl.BlockSpec((1, LANES), lambda i: (0, 0))],
        out_specs=[pl.BlockSpec((tm, d), lambda i: (i, 0)),
                   pl.BlockSpec((tm, LANES), lambda i: (i, 0))],
        out_shape=[jax.ShapeDtypeStruct((n, d), BF16), jax.ShapeDtypeStruct((n, LANES), F32)],
        compiler_params=_params("parallel"),
        name="moe_router",
    )(x, g.reshape(1, d), w_cat, b_cat)


def _moe_expert_kernel(h_ref, gates_ref, x_ref, wg_ref, wu_ref, wd_ref, gf_ref, o_ref, *, final_norm):
    e = pl.program_id(1)

    @pl.when(e == 0)
    def _():
        o_ref[...] = x_ref[...]

    h = h_ref[...]
    a = jnp.dot(h, wg_ref[0], preferred_element_type=F32)
    b = jnp.dot(h, wu_ref[0], preferred_element_type=F32)
    gates = gates_ref[...]
    lane = lax.broadcasted_iota(jnp.int32, gates.shape, 1)
    gcol = jnp.sum(jnp.where(lane == e, gates, 0.0), axis=-1, keepdims=True)
    hid = (a * _sigmoid(a)) * b * gcol
    o_ref[...] += jnp.dot(hid.astype(BF16), wd_ref[0], preferred_element_type=F32)

    if final_norm:
        @pl.when(e == MOE_EXPERTS - 1)
        def _():
            o_ref[...] = _rms(o_ref[...], gf_ref[...])


def _moe_experts(h, gates, x, wg, wu, wd, g_final, tm, final_norm):
    n, d = x.shape
    return pl.pallas_call(
        functools.partial(_moe_expert_kernel, final_norm=final_norm),
        grid=(n // tm, MOE_EXPERTS),
        in_specs=[pl.BlockSpec((tm, d), lambda i, e: (i, 0)),
                  pl.BlockSpec((tm, LANES), lambda i, e: (i, 0)),
                  pl.BlockSpec((tm, d), lambda i, e: (i, 0)),
                  pl.BlockSpec((1, d, MOE_FF), lambda i, e: (e, 0, 0)),
                  pl.BlockSpec((1, d, MOE_FF), lambda i, e: (e, 0, 0)),
                  pl.BlockSpec((1, MOE_FF, d), lambda i, e: (e, 0, 0)),
                  pl.BlockSpec((1, d), lambda i, e: (0, 0))],
        out_specs=pl.BlockSpec((tm, d), lambda i, e: (i, 0)),
        out_shape=jax.ShapeDtypeStruct((n, d), F32),
        compiler_params=_params("parallel", "arbitrary"),
        name="moe_experts",
    )(h, gates, x, wg, wu, wd, g_final.reshape(1, d))


def _hier_moe(x, g_norm, w_cat, b_cat, wg, wu, wd, g_final, tm, final_norm):
    h, gates = _moe_router(x, g_norm, w_cat, b_cat, tm)
    return _moe_experts(h, gates, x, wg, wu, wd, g_final, tm, final_norm)


def _moba_prompt_kernel(q_ref, k_ref, v_ref, o_ref, *, t_len):
    nb = t_len // MOBA_BLOCK
    blk = MOBA_BLOCK
    scale = ATT_HEAD_DIM ** -0.5
    kf = k_ref[...]
    kb = kf.astype(BF16)
    vb = v_ref[...].astype(BF16)
    kmean = jnp.mean(kf.reshape(nb, blk, ATT_HEAD_DIM), axis=1)
    row = lax.broadcasted_iota(jnp.int32, (blk, blk), 0)
    col = lax.broadcasted_iota(jnp.int32, (blk, blk), 1)
    causal = col <= row
    for i in range(nb):
        qi = q_ref[i * blk:(i + 1) * blk, :]
        qi_b = qi.astype(BF16)
        sel = [None] * i
        if i > MOBA_TOPK:
            gate = lax.dot_general(qi, kmean, NT_DIMS, precision=HIGHEST, preferred_element_type=F32)
            g = [gate[:, j:j + 1] for j in range(i)]
            rank = [jnp.zeros((blk, 1), F32) for _ in range(i)]
            for lo_j in range(i):
                for hi_j in range(lo_j + 1, i):
                    lo_wins = jnp.where(g[lo_j] >= g[hi_j], 1.0, 0.0)
                    rank[hi_j] = rank[hi_j] + lo_wins
                    rank[lo_j] = rank[lo_j] + (1.0 - lo_wins)
            sel = [r < float(MOBA_TOPK) for r in rank]
        pieces = []
        for j in range(i + 1):
            s = lax.dot_general(qi_b, kb[j * blk:(j + 1) * blk], NT_DIMS, preferred_element_type=F32) * scale
            if j == i:
                s = jnp.where(causal, s, NEG_INF)
            elif sel[j] is not None:
                s = jnp.where(sel[j], s, NEG_INF)
            pieces.append(s)
        m = jnp.max(pieces[0], axis=-1, keepdims=True)
        for s in pieces[1:]:
            m = jnp.maximum(m, jnp.max(s, axis=-1, keepdims=True))
        l = jnp.zeros((blk, 1), F32)
        acc = jnp.zeros((blk, ATT_HEAD_DIM), F32)
        for j, s in enumerate(pieces):
            p = jnp.exp(s - m)
            l = l + jnp.sum(p, axis=-1, keepdims=True)
            acc = acc + jnp.dot(p.astype(BF16), vb[j * blk:(j + 1) * blk], preferred_element_type=F32)
        o_ref[i * blk:(i + 1) * blk, :] = acc / l


def _moba_prompt(q, k, v, n_seq, t_len):
    assert t_len % MOBA_BLOCK == 0
    spec = pl.BlockSpec((t_len, ATT_HEAD_DIM), lambda b, h: (b, h))
    return pl.pallas_call(
        functools.partial(_moba_prompt_kernel, t_len=t_len),
        grid=(n_seq, ATT_HEADS),
        in_specs=[spec, spec, spec],
        out_specs=spec,
        out_shape=jax.ShapeDtypeStruct(q.shape, F32),
        compiler_params=_params("parallel", "parallel"),
        name="moba_prompt",
    )(q, k, v)


KMEAN_PAGES = 8


def _kmean_kernel(pt_ref, *refs):
    page_refs, o_ref = refs[:KMEAN_PAGES], refs[KMEAN_PAGES]
    for b in range(KMEAN_PAGES // PAGES_PER_BLOCK):
        s = jnp.sum(page_refs[PAGES_PER_BLOCK * b][0], axis=0)
        for r in range(1, PAGES_PER_BLOCK):
            s = s + jnp.sum(page_refs[PAGES_PER_BLOCK * b + r][0], axis=0)
        o_ref[0, b] = s / MOBA_BLOCK


def _cache_kmean(cache_k, page_table_flat, n_req, n_pages):
    steps = n_pages // KMEAN_PAGES
    blocks_per_step = KMEAN_PAGES // PAGES_PER_BLOCK

    def page_spec(r):
        return pl.BlockSpec((1, PAGE_SIZE, ATT_HEADS, ATT_HEAD_DIM),
                            lambda b, s, pt: (pt[b * n_pages + s * KMEAN_PAGES + r], 0, 0, 0))

    return pl.pallas_call(
        _kmean_kernel,
        grid_spec=pltpu.PrefetchScalarGridSpec(
            num_scalar_prefetch=1,
            grid=(n_req, steps),
            in_specs=[page_spec(r) for r in range(KMEAN_PAGES)],
            out_specs=pl.BlockSpec((1, blocks_per_step, ATT_HEADS, ATT_HEAD_DIM), lambda b, s, pt: (b, s, 0, 0)),
        ),
        out_shape=jax.ShapeDtypeStruct((n_req, n_pages // PAGES_PER_BLOCK, ATT_HEADS, ATT_HEAD_DIM), F32),
        compiler_params=_params("parallel", "parallel"),
        name="cache_kmean",
    )(page_table_flat, *([cache_k] * KMEAN_PAGES))


def _top_blocks_kernel(q_ref, kmean_ref, idx_ref):
    q = q_ref[0]
    gate = jnp.sum(kmean_ref[0] * q[None, :, :], axis=-1, keepdims=True)
    n_blocks = gate.shape[0]
    blk_id = lax.broadcasted_iota(jnp.int32, gate.shape, 0)
    lane = lax.broadcasted_iota(jnp.int32, (ATT_HEADS, LANES), 1)
    out = jnp.zeros((ATT_HEADS, LANES), jnp.int32)
    for r in range(MOBA_TOPK):
        m = jnp.max(gate, axis=0, keepdims=True)
        idx = jnp.min(jnp.where(gate == m, blk_id, n_blocks), axis=0, keepdims=True)
        gate = jnp.where(blk_id == idx, -jnp.inf, gate)
        out = jnp.where(lane == r, idx[0], out)
    idx_ref[0] = out


def _top_blocks(q3, kmean):
    n_req, n_blocks = kmean.shape[:2]
    return pl.pallas_call(
        _top_blocks_kernel,
        grid=(n_req,),
        in_specs=[pl.BlockSpec((1, ATT_HEADS, ATT_HEAD_DIM), lambda b: (b, 0, 0)),
                  pl.BlockSpec((1, n_blocks, ATT_HEADS, ATT_HEAD_DIM), lambda b: (b, 0, 0, 0))],
        out_specs=pl.BlockSpec((1, ATT_HEADS, LANES), lambda b: (b, 0, 0)),
        out_shape=jax.ShapeDtypeStruct((n_req, ATT_HEADS, LANES), jnp.int32),
        compiler_params=_params("parallel"),
        name="moba_top_blocks",
    )(q3, kmean)


SEL_PAGES = MOBA_TOPK * PAGES_PER_BLOCK


def _moba_sample_kernel(idx_ref, pt_ref, q_ref, kn_ref, vn_ref, *refs):
    k_refs, v_refs, o_ref = refs[:SEL_PAGES], refs[SEL_PAGES:2 * SEL_PAGES], refs[2 * SEL_PAGES]
    scale = ATT_HEAD_DIM ** -0.5
    q = q_ref[0]
    q8 = jnp.broadcast_to(q, (SUBLANES, ATT_HEAD_DIM)).astype(BF16)
    s_own = jnp.sum(q * kn_ref[0], axis=-1, keepdims=True) * scale
    scores = [lax.dot_general(q8, kr[0].astype(BF16), NT_DIMS, preferred_element_type=F32) * scale for kr in k_refs]
    m = s_own
    for s in scores:
        m = jnp.maximum(m, jnp.max(s, axis=-1, keepdims=True))
    p_own = jnp.exp(s_own - m)
    l = p_own
    acc = p_own * vn_ref[0]
    for s, vr in zip(scores, v_refs):
        p = jnp.exp(s - m)
        l = l + jnp.sum(p, axis=-1, keepdims=True)
        acc = acc + jnp.dot(p.astype(BF16), vr[0].astype(BF16), preferred_element_type=F32)
    o_ref[0] = (acc / l)[0:1, :]


def _moba_sample(q3, k3, v3, cache_k3, cache_v3, idx_flat, page_table_flat, n_req, n_pages):
    def page_spec(sel, r):
        def index_map(b, h, idx, pt):
            blk = idx[(b * ATT_HEADS + h) * MOBA_TOPK + sel]
            return (pt[b * n_pages + blk * PAGES_PER_BLOCK + r], 0, h)
        return pl.BlockSpec((1, PAGE_SIZE, ATT_HEAD_DIM), index_map)

    row_spec = pl.BlockSpec((1, 1, ATT_HEAD_DIM), lambda b, h, idx, pt: (b, 0, h))
    page_specs = [page_spec(sel, r) for sel in range(MOBA_TOPK) for r in range(PAGES_PER_BLOCK)]
    return pl.pallas_call(
        _moba_sample_kernel,
        grid_spec=pltpu.PrefetchScalarGridSpec(
            num_scalar_prefetch=2,
            grid=(n_req, ATT_HEADS),
            in_specs=[row_spec, row_spec, row_spec] + page_specs + page_specs,
            out_specs=row_spec,
        ),
        out_shape=jax.ShapeDtypeStruct(q3.shape, F32),
        compiler_params=_params("parallel", "parallel"),
        name="moba_sample",
    )(idx_flat, page_table_flat, q3, k3, v3, *([cache_k3] * SEL_PAGES), *([cache_v3] * SEL_PAGES))


def _pair_block_diag(w):
    hds, d, _ = w.shape
    wp = w.reshape(hds // 2, 2, d, d)
    z = jnp.zeros((hds // 2, d, d), w.dtype)
    top = jnp.concatenate([wp[:, 0], z], axis=-1)
    bot = jnp.concatenate([z, wp[:, 1]], axis=-1)
    return jnp.concatenate([top, bot], axis=-2)


def _lane_pad(vec, offset):
    out = jnp.zeros((1, LANES), F32)
    return out.at[0, offset:offset + vec.shape[0]].set(vec.astype(F32))


def kernel(x_prompt, x_sample, state_conv, state_rglru_h, state_gdn, cache_k, cache_v, page_table, norm_mix, norm_ffn, norm_final, w_in0, conv0_w, conv0_b, rg_wa, rg_ba, rg_wi, rg_bi, rg_lambda, gdn_a_log, gdn_dt_bias, gdn_norm, w_out0, w_qkv1, w_out1, moe_w_group, moe_b_group, moe_w_router, moe_b_router, moe_w_gate, moe_w_up, moe_w_down):
    bp, tp, d = x_prompt.shape
    bs, ts, _ = x_sample.shape
    n_pool = cache_k.shape[0]
    n_pages = page_table.shape[1]
    assert ts == 1 and d == D_MODEL
    assert n_pages % PAGES_PER_BLOCK == 0 and n_pages % KMEAN_PAGES == 0
    assert n_pages // PAGES_PER_BLOCK >= MOBA_TOPK

    w_in0_b = jnp.pad(w_in0, ((0, 0), (0, IN0_PAD - IN0_DIM))).astype(BF16)
    w_out0_b = w_out0.astype(BF16)
    w_qkv1_b = w_qkv1.astype(BF16)
    w_out1_b = w_out1.astype(BF16)
    wg_b, wu_b, wd_b = moe_w_gate.astype(BF16), moe_w_up.astype(BF16), moe_w_down.astype(BF16)
    mix_w = (conv0_w, conv0_b.reshape(1, CONV_CH),
             _pair_block_diag(rg_wa).astype(BF16), rg_ba.reshape(1, RG_WIDTH),
             _pair_block_diag(rg_wi).astype(BF16), rg_bi.reshape(1, RG_WIDTH),
             rg_lambda.reshape(1, RG_WIDTH), _lane_pad(gdn_a_log, GDN_HEADS), _lane_pad(gdn_dt_bias, GDN_HEADS),
             gdn_norm.reshape(1, GDN_DV), w_out0_b)
    router_w = [jnp.pad(jnp.concatenate([moe_w_router[l], moe_w_group[l]], axis=-1),
                        ((0, 0), (0, LANES - MOE_EXPERTS - MOE_GROUPS))) for l in range(2)]
    router_b = [_lane_pad(jnp.concatenate([moe_b_router[l], moe_b_group[l]]), 0) for l in range(2)]

    xp = x_prompt.reshape(bp * tp, d)
    xs8 = jnp.pad(x_sample, ((0, 0), (0, SUBLANES - ts), (0, 0))).reshape(bs * SUBLANES, d)

    (proj_p,) = _norm_matmul(xp, norm_mix[0], w_in0_b, (IN0_PAD,), tm=512)
    (proj_s,) = _norm_matmul(xs8, norm_mix[0], w_in0_b, (IN0_PAD,), tm=bs * SUBLANES)
    xp, p_h, p_gdn = _mixer(proj_p, xp,
                            jnp.zeros((bp, SUBLANES, CONV_CH), F32), jnp.zeros((bp, 1, RG_WIDTH), F32),
                            jnp.zeros((bp, GDN_HEADS, GDN_DK, GDN_DV), F32), mix_w,
                            n_seq=bp, t_len=tp, tc=256, chunk=GDN_CHUNK, t_valid=tp)
    cbuf_s = jnp.pad(state_conv, ((0, 0), (SUBLANES - (CONV_W - 1), 0), (0, 0)))
    xs8, s_h, s_gdn = _mixer(proj_s, xs8, cbuf_s, state_rglru_h.reshape(bs, 1, RG_WIDTH), state_gdn, mix_w,
                             n_seq=bs, t_len=SUBLANES, tc=SUBLANES, chunk=SUBLANES, t_valid=ts)
    xs = xs8.reshape(bs, SUBLANES, d)[:, 0]
    p_conv = proj_p.reshape(bp, tp, IN0_PAD)[:, tp - (CONV_W - 1):, :CONV_CH]
    s_conv = jnp.concatenate([state_conv[:, ts:], proj_s.reshape(bs, SUBLANES, IN0_PAD)[:, :ts, :CONV_CH]], axis=1)

    moe0 = (norm_ffn[0], router_w[0], router_b[0], wg_b[0], wu_b[0], wd_b[0], norm_final)
    xp = _hier_moe(xp, *moe0, tm=512, final_norm=False)
    xs = _hier_moe(xs, *moe0, tm=bs, final_norm=False)

    hd_all = ATT_HEADS * ATT_HEAD_DIM
    qp, kp, vp = _norm_matmul(xp, norm_mix[1], w_qkv1_b, (hd_all,) * 3, tm=512)
    op = _moba_prompt(qp, kp, vp, bp, tp)
    xp = _matmul_residual(op, w_out1_b, xp, tm=512)

    qs, ks, vs = _norm_matmul(xs, norm_mix[1], w_qkv1_b, (hd_all,) * 3, tm=bs)
    pt_flat = page_table.reshape(-1)
    kmean = _cache_kmean(cache_k, pt_flat, bs, n_pages)
    idx = _top_blocks(qs.reshape(bs, ATT_HEADS, ATT_HEAD_DIM), kmean)
    row3 = lambda a: a.reshape(bs, 1, hd_all)
    os_ = _moba_sample(row3(qs), row3(ks), row3(vs),
                       cache_k.reshape(n_pool, PAGE_SIZE, hd_all), cache_v.reshape(n_pool, PAGE_SIZE, hd_all),
                       idx[:, :, :MOBA_TOPK].reshape(-1), pt_flat, bs, n_pages)
    xs = _matmul_residual(os_.reshape(bs, hd_all), w_out1_b, xs, tm=bs)

    moe1 = (norm_ffn[1], router_w[1], router_b[1], wg_b[1], wu_b[1], wd_b[1], norm_final)
    y_p = _hier_moe(xp, *moe1, tm=512, final_norm=True)
    y_s = _hier_moe(xs, *moe1, tm=bs, final_norm=True)

    heads = lambda a, b_, t_: a.reshape(b_, t_, ATT_HEADS, ATT_HEAD_DIM)
    return (y_p.reshape(bp, tp, d), y_s.reshape(bs, ts, d),
            p_conv, p_h.reshape(bp, RG_WIDTH), p_gdn,
            heads(kp, bp, tp), heads(vp, bp, tp),
            s_conv, s_h.reshape(bs, RG_WIDTH), s_gdn,
            heads(ks, bs, ts), heads(vs, bs, ts))
```

```python
import functools

import jax
import jax.numpy as jnp
from jax import lax
from jax.experimental import pallas as pl
from jax.experimental.pallas import tpu as pltpu

F32 = jnp.float32
BF16 = jnp.bfloat16
HIGHEST = lax.Precision.HIGHEST

D_MODEL = 1024
RG_WIDTH = 512
RG_HEADS = 8
RG_HEAD_DIM = 64
RG_C = 8.0
GDN_HEADS = 4
GDN_DK = 128
GDN_DV = 128
GDN_CHUNK = 64
CONV_W = 4
CONV_CH = 2048
IN0_DIM = 3080
IN0_PAD = 3200
ATT_HEADS = 8
ATT_HEAD_DIM = 128
MOBA_BLOCK = 256
MOBA_TOPK = 3
PAGE_SIZE = 128
PAGES_PER_BLOCK = MOBA_BLOCK // PAGE_SIZE
MOE_GROUPS = 4
MOE_EXPERTS_PER_GROUP = 4
MOE_EXPERTS = 16
MOE_FF = 512
RMS_EPS = 1e-6
L2_EPS = 1e-6
NEG_INF = -1e30

LANES = 128
SUBLANES = 8
VMEM_LIMIT = 56 * 1024 * 1024

NT_DIMS = (((1,), (1,)), ((), ()))
TN_DIMS = (((0,), (0,)), ((), ()))


def _params(*sem):
    return pltpu.CompilerParams(dimension_semantics=sem, vmem_limit_bytes=VMEM_LIMIT)


def _rms(x, g):
    return x * lax.rsqrt(jnp.mean(x * x, axis=-1, keepdims=True) + RMS_EPS) * g


def _softplus(x):
    return jnp.maximum(x, 0.0) + jnp.log1p(jnp.exp(-jnp.abs(x)))


def _sigmoid(x):
    return 1.0 / (1.0 + jnp.exp(-x))


def _norm_matmul_kernel(x_ref, g_ref, w_ref, *out_refs, widths):
    h = _rms(x_ref[...], g_ref[...]).astype(BF16)
    lo = 0
    for o_ref, width in zip(out_refs, widths):
        o_ref[...] = jnp.dot(h, w_ref[:, lo:lo + width], preferred_element_type=F32)
        lo += width


def _norm_matmul(x, g, w_bf16, widths, tm):
    n, d = x.shape
    assert n % tm == 0 and sum(widths) == w_bf16.shape[1]
    return pl.pallas_call(
        functools.partial(_norm_matmul_kernel, widths=widths),
        grid=(n // tm,),
        in_specs=[pl.BlockSpec((tm, d), lambda i: (i, 0)),
                  pl.BlockSpec((1, d), lambda i: (0, 0)),
                  pl.BlockSpec(w_bf16.shape, lambda i: (0, 0))],
        out_specs=[pl.BlockSpec((tm, wd), lambda i: (i, 0)) for wd in widths],
        out_shape=[jax.ShapeDtypeStruct((n, wd), F32) for wd in widths],
        compiler_params=_params("parallel"),
        name="norm_matmul",
    )(x, g.reshape(1, d), w_bf16)


def _matmul_residual_kernel(a_ref, w_ref, r_ref, o_ref):
    o_ref[...] = r_ref[...] + jnp.dot(a_ref[...].astype(BF16), w_ref[...], preferred_element_type=F32)


def _matmul_residual(a, w_bf16, res, tm):
    n, k = a.shape
    m = w_bf16.shape[1]
    assert n % tm == 0
    return pl.pallas_call(
        _matmul_residual_kernel,
        grid=(n // tm,),
        in_specs=[pl.BlockSpec((tm, k), lambda i: (i, 0)),
                  pl.BlockSpec((k, m), lambda i: (0, 0)),
                  pl.BlockSpec((tm, m), lambda i: (i, 0))],
        out_specs=pl.BlockSpec((tm, m), lambda i: (i, 0)),
        out_shape=jax.ShapeDtypeStruct((n, m), F32),
        compiler_params=_params("parallel"),
        name="matmul_residual",
    )(a, w_bf16, res)


def _linear_scan(a, u, n_rows):
    row = lax.broadcasted_iota(jnp.int32, a.shape, 0)
    d = 1
    while d < n_rows:
        a_prev = jnp.where(row >= d, pltpu.roll(a, d, 0), 1.0)
        u_prev = jnp.where(row >= d, pltpu.roll(u, d, 0), 0.0)
        u = a * u_prev + u
        a = a * a_prev
        d *= 2
    return a, u


def _bmm(a, b):
    return jnp.einsum('bij,bjk->bik', a.astype(BF16), b.astype(BF16), preferred_element_type=F32)


def _unit_lower_solve(lmat, rhs, c):
    m = -lmat
    sol = rhs + _bmm(m, rhs)
    p = 2
    while p < c:
        m = _bmm(m, m)
        sol = sol + _bmm(m, sol)
        p *= 2
    return sol


def _mixer_kernel(conv_ref, gate_ref, z_ref, ba_ref, x_ref, cbuf0_ref, h0_ref, s0_ref,
                  cw_ref, cb_ref, wa_ref, rba_ref, wi_ref, rbi_ref, lam_ref, alog_ref, dtb_ref, gnorm_ref, wout_ref,
                  x1_ref, ht_ref, st_ref,
                  cbuf, hcar, state, act, *, tc, chunk, t_valid, nt):
    t = pl.program_id(1)

    @pl.when(t == 0)
    def _():
        cbuf[0:SUBLANES, :] = cbuf0_ref[0]
        hcar[...] = h0_ref[0]
        state[...] = s0_ref[0]

    cbuf[SUBLANES:SUBLANES + tc, :] = conv_ref[...]
    base = SUBLANES - (CONV_W - 1)
    y = cb_ref[...] + cbuf[base:base + tc, :] * cw_ref[0:1, :]
    for j in range(1, CONV_W):
        y = y + cbuf[base + j:base + j + tc, :] * cw_ref[j:j + 1, :]
    cbuf[0:SUBLANES, :] = cbuf[tc:tc + SUBLANES, :]

    masked = t_valid < nt * tc
    if masked:
        valid = (t * tc + lax.broadcasted_iota(jnp.int32, (tc, 1), 0)) < t_valid

    xr = y[:, :RG_WIDTH]
    xr_b = xr.astype(BF16)
    pair = 2 * RG_HEAD_DIM
    r_lin = jnp.concatenate(
        [jnp.dot(xr_b[:, pair * i:pair * (i + 1)], wa_ref[i], preferred_element_type=F32) for i in range(RG_WIDTH // pair)], axis=-1)
    i_lin = jnp.concatenate(
        [jnp.dot(xr_b[:, pair * i:pair * (i + 1)], wi_ref[i], preferred_element_type=F32) for i in range(RG_WIDTH // pair)], axis=-1)
    r = _sigmoid(r_lin + rba_ref[...])
    gi = _sigmoid(i_lin + rbi_ref[...])
    log_a = -RG_C * r * _softplus(-lam_ref[...])
    a = jnp.exp(log_a)
    u = jnp.sqrt(-jnp.tanh(log_a) * (a * a + 1.0)) * (gi * xr)
    if masked:
        a = jnp.where(valid, a, 1.0)
        u = jnp.where(valid, u, 0.0)
    a_cum, h_loc = _linear_scan(a, u, tc)
    h = h_loc + a_cum * hcar[...]
    hcar[...] = h[tc - 1:tc, :]
    act[:, 0:RG_WIDTH] = (h * jax.nn.gelu(gate_ref[...])).astype(BF16)

    qkv = y[:, RG_WIDTH:]
    qkv = qkv * _sigmoid(qkv)
    ba = ba_ref[...]
    beta_all = _sigmoid(ba)
    g_all = -jnp.exp(alog_ref[...]) * _softplus(ba + dtb_ref[...])
    if masked:
        beta_all = jnp.where(valid, beta_all, 0.0)
        g_all = jnp.where(valid, g_all, 0.0)
    in_chunk = lax.broadcasted_iota(jnp.int32, g_all.shape, 0) & (chunk - 1)
    gcum_all = g_all
    d = 1
    while d < chunk:
        gcum_all = gcum_all + jnp.where(in_chunk >= d, pltpu.roll(gcum_all, d, 0), 0.0)
        d *= 2

    nc = tc // chunk

    def chunk_head_batch(cols_of_head):
        per_head = [cols_of_head(hd) for hd in range(GDN_HEADS)]
        return jnp.concatenate([per_head[hd][ci * chunk:(ci + 1) * chunk][None]
                                for ci in range(nc) for hd in range(GDN_HEADS)], axis=0)

    def l2_normalised(x):
        return x * lax.rsqrt(jnp.sum(x * x, axis=-1, keepdims=True) + L2_EPS)

    q_b = chunk_head_batch(lambda hd: l2_normalised(qkv[:, GDN_DK * hd:GDN_DK * (hd + 1)]) * (GDN_DK ** -0.5))
    k_b = chunk_head_batch(lambda hd: l2_normalised(qkv[:, 512 + GDN_DK * hd:512 + GDN_DK * (hd + 1)]))
    v_b = chunk_head_batch(lambda hd: qkv[:, 1024 + GDN_DV * hd:1024 + GDN_DV * (hd + 1)])
    beta_b = chunk_head_batch(lambda hd: beta_all[:, hd:hd + 1])
    gcol = chunk_head_batch(lambda hd: gcum_all[:, GDN_HEADS + hd:GDN_HEADS + hd + 1])

    nbatch = nc * GDN_HEADS
    crow = lax.broadcasted_iota(jnp.int32, (nbatch, chunk, chunk), 1)
    ccol = lax.broadcasted_iota(jnp.int32, (nbatch, chunk, chunk), 2)
    lower = crow >= ccol
    strict = crow > ccol
    grow = jnp.sum(jnp.where(crow == ccol, gcol, 0.0), axis=1, keepdims=True)
    decay = jnp.where(lower, jnp.exp(jnp.where(lower, gcol - grow, 0.0)), 0.0)
    kb = k_b * beta_b
    k_bf = k_b.astype(BF16)
    kk = jnp.einsum('bik,bjk->bij', kb.astype(BF16), k_bf, preferred_element_type=F32)
    rhs = jnp.concatenate([v_b * beta_b, kb * jnp.exp(gcol)], axis=-1)
    sol = _unit_lower_solve(jnp.where(strict, kk * decay, 0.0), rhs, chunk)
    qk = jnp.where(lower, jnp.einsum('bik,bjk->bij', q_b.astype(BF16), k_bf, preferred_element_type=F32) * decay, 0.0)
    qk_bf = qk.astype(BF16)
    q_dec = (q_b * jnp.exp(gcol)).astype(BF16)
    g_last = gcol[:, chunk - 1:chunk, :]
    k_dec = (k_b * jnp.exp(g_last - gcol)).astype(BF16)
    s_decay = jnp.exp(g_last)
    u_all = sol[:, :, :GDN_DV]
    w_bf = sol[:, :, GDN_DV:].astype(BF16)

    gnorm = gnorm_ref[...]
    z_all = z_ref[...]
    s_heads = [state[hd] for hd in range(GDN_HEADS)]
    for ci in range(nc):
        lo = ci * chunk
        for hd in range(GDN_HEADS):
            bi = ci * GDN_HEADS + hd
            s_h = s_heads[hd]
            s_bf = s_h.astype(BF16)
            v_new = u_all[bi] - jnp.dot(w_bf[bi], s_bf, preferred_element_type=F32)
            v_new_bf = v_new.astype(BF16)
            o = (jnp.dot(q_dec[bi], s_bf, preferred_element_type=F32)
                 + jnp.dot(qk_bf[bi], v_new_bf, preferred_element_type=F32))
            s_heads[hd] = s_h * s_decay[bi] + lax.dot_general(k_dec[bi], v_new_bf, TN_DIMS, preferred_element_type=F32)
            zh = z_all[lo:lo + chunk, GDN_DV * hd:GDN_DV * (hd + 1)]
            o = _rms(o, gnorm) * (zh * _sigmoid(zh))
            act[lo:lo + chunk, RG_WIDTH + GDN_DV * hd:RG_WIDTH + GDN_DV * (hd + 1)] = o.astype(BF16)
    for hd in range(GDN_HEADS):
        state[hd] = s_heads[hd]

    x1_ref[...] = x_ref[...] + jnp.dot(act[...], wout_ref[...], preferred_element_type=F32)

    @pl.when(t == nt - 1)
    def _():
        ht_ref[0] = hcar[...]
        st_ref[0] = state[...]


def _mixer(proj, x, cbuf0, h0, s0, wts, *, n_seq, t_len, tc, chunk, t_valid):
    nt = t_len // tc
    assert t_len % tc == 0 and tc % chunk == 0 and tc % SUBLANES == 0
    row = lambda b, t: b * nt + t
    const2 = lambda b, t: (0, 0)
    (cw, cb, wa4, rba, wi4, rbi, lam, alog, dtb, gnorm, wout) = wts
    in_specs = [
        pl.BlockSpec((tc, CONV_CH), lambda b, t: (row(b, t), 0)),
        pl.BlockSpec((tc, RG_WIDTH), lambda b, t: (row(b, t), CONV_CH // RG_WIDTH)),
        pl.BlockSpec((tc, RG_WIDTH), lambda b, t: (row(b, t), CONV_CH // RG_WIDTH + 1)),
        pl.BlockSpec((tc, LANES), lambda b, t: (row(b, t), (CONV_CH + 2 * RG_WIDTH) // LANES)),
        pl.BlockSpec((tc, D_MODEL), lambda b, t: (row(b, t), 0)),
        pl.BlockSpec((1, SUBLANES, CONV_CH), lambda b, t: (b, 0, 0)),
        pl.BlockSpec((1, 1, RG_WIDTH), lambda b, t: (b, 0, 0)),
        pl.BlockSpec((1, GDN_HEADS, GDN_DK, GDN_DV), lambda b, t: (b, 0, 0, 0)),
        pl.BlockSpec(cw.shape, const2), pl.BlockSpec(cb.shape, const2),
        pl.BlockSpec(wa4.shape, lambda b, t: (0, 0, 0)), pl.BlockSpec(rba.shape, const2),
        pl.BlockSpec(wi4.shape, lambda b, t: (0, 0, 0)), pl.BlockSpec(rbi.shape, const2),
        pl.BlockSpec(lam.shape, const2), pl.BlockSpec(alog.shape, const2), pl.BlockSpec(dtb.shape, const2),
        pl.BlockSpec(gnorm.shape, const2), pl.BlockSpec(wout.shape, const2),
    ]
    out_specs = [
        pl.BlockSpec((tc, D_MODEL), lambda b, t: (row(b, t), 0)),
        pl.BlockSpec((1, 1, RG_WIDTH), lambda b, t: (b, 0, 0)),
        pl.BlockSpec((1, GDN_HEADS, GDN_DK, GDN_DV), lambda b, t: (b, 0, 0, 0)),
    ]
    out_shape = [
        jax.ShapeDtypeStruct((n_seq * t_len, D_MODEL), F32),
        jax.ShapeDtypeStruct((n_seq, 1, RG_WIDTH), F32),
        jax.ShapeDtypeStruct((n_seq, GDN_HEADS, GDN_DK, GDN_DV), F32),
    ]
    return pl.pallas_call(
        functools.partial(_mixer_kernel, tc=tc, chunk=chunk, t_valid=t_valid, nt=nt),
        grid=(n_seq, nt),
        in_specs=in_specs, out_specs=out_specs, out_shape=out_shape,
        scratch_shapes=[pltpu.VMEM((tc + SUBLANES, CONV_CH), F32),
                        pltpu.VMEM((1, RG_WIDTH), F32),
                        pltpu.VMEM((GDN_HEADS, GDN_DK, GDN_DV), F32),
                        pltpu.VMEM((tc, D_MODEL), BF16)],
        compiler_params=_params("arbitrary", "arbitrary"),
        name="mixer0",
    )(proj, proj, proj, proj, x, cbuf0, h0, s0, cw, cb, wa4, rba, wi4, rbi, lam, alog, dtb, gnorm, wout)


def _router_kernel(x_ref, g_ref, w_ref, b_ref, h_ref, gates_ref):
    h = _rms(x_ref[...], g_ref[...])
    h_ref[...] = h.astype(BF16)
    logits = jnp.dot(h, w_ref[...], precision=HIGHEST, preferred_element_type=F32) + b_ref[...]
    lane = lax.broadcasted_iota(jnp.int32, logits.shape, 1)
    big = jnp.int32(1 << 20)
    is_grp = (lane >= MOE_EXPERTS) & (lane < MOE_EXPERTS + MOE_GROUPS)
    gl = jnp.where(is_grp, logits, -jnp.inf)
    gmax = jnp.max(gl, axis=-1, keepdims=True)
    gidx = jnp.min(jnp.where(gl == gmax, lane, big), axis=-1, keepdims=True) - MOE_EXPERTS
    g_top = 1.0 / jnp.sum(jnp.exp(gl - gmax), axis=-1, keepdims=True)
    in_grp = (lane >= gidx * MOE_EXPERTS_PER_GROUP) & (lane < (gidx + 1) * MOE_EXPERTS_PER_GROUP)
    el = jnp.where(in_grp, logits, -jnp.inf)
    m1 = jnp.max(el, axis=-1, keepdims=True)
    i1 = jnp.min(jnp.where(el == m1, lane, big), axis=-1, keepdims=True)
    el2 = jnp.where(lane == i1, -jnp.inf, el)
    m2 = jnp.max(el2, axis=-1, keepdims=True)
    i2 = jnp.min(jnp.where(el2 == m2, lane, big), axis=-1, keepdims=True)
    e2 = jnp.exp(m2 - m1)
    w1 = g_top / (1.0 + e2)
    w2 = g_top * e2 / (1.0 + e2)
    gates_ref[...] = jnp.where(lane == i1, w1, 0.0) + jnp.where(lane == i2, w2, 0.0)


def _moe_router(x, g, w_cat, b_cat, tm):
    n, d = x.shape
    return pl.pallas_call(
        _router_kernel,
        grid=(n // tm,),
        in_specs=[pl.BlockSpec((tm, d), lambda i: (i, 0)),
                  pl.BlockSpec((1, d), lambda i: (0, 0)),
                  pl.BlockSpec((d, LANES), lambda i: (0, 0)),
                  pl.BlockSpec((1, LANES), lambda i: (0, 0))],
        out_specs=[pl.BlockSpec((tm, d), lambda i: (i, 0)),
                   pl.BlockSpec((tm, LANES), lambda i: (i, 0))],
        out_shape=[jax.ShapeDtypeStruct((n, d), BF16), jax.ShapeDtypeStruct((n, LANES), F32)],
        compiler_params=_params("parallel"),
        name="moe_router",
    )(x, g.reshape(1, d), w_cat, b_cat)


def _block_means(page_refs, o_ref):
    for b in range(len(page_refs) // PAGES_PER_BLOCK):
        s = jnp.sum(page_refs[PAGES_PER_BLOCK * b][0], axis=0)
        for r in range(1, PAGES_PER_BLOCK):
            s = s + jnp.sum(page_refs[PAGES_PER_BLOCK * b + r][0], axis=0)
        o_ref[0, b] = s / MOBA_BLOCK


def _moe_expert_kernel(*refs, final_norm, n_cache_pages):
    if n_cache_pages:
        refs = refs[1:]
    h_ref, gates_ref, x_ref, wg_ref, wu_ref, wd_ref, gf_ref = refs[:7]
    page_refs = refs[7:7 + n_cache_pages]
    o_ref = refs[7 + n_cache_pages]
    e = pl.program_id(1)

    if n_cache_pages:
        _block_means(page_refs, refs[8 + n_cache_pages])

    @pl.when(e == 0)
    def _():
        o_ref[...] = x_ref[...]

    h = h_ref[...]
    a = jnp.dot(h, wg_ref[0], preferred_element_type=F32)
    b = jnp.dot(h, wu_ref[0], preferred_element_type=F32)
    gates = gates_ref[...]
    lane = lax.broadcasted_iota(jnp.int32, gates.shape, 1)
    gcol = jnp.sum(jnp.where(lane == e, gates, 0.0), axis=-1, keepdims=True)
    hid = (a * _sigmoid(a)) * b * gcol
    o_ref[...] += jnp.dot(hid.astype(BF16), wd_ref[0], preferred_element_type=F32)

    if final_norm:
        @pl.when(e == MOE_EXPERTS - 1)
        def _():
            o_ref[...] = _rms(o_ref[...], gf_ref[...])


def _moe_experts(h, gates, x, wg, wu, wd, g_final, tm, final_norm, cache=None):
    n, d = x.shape
    grid = (n // tm, MOE_EXPERTS)
    in_specs = [pl.BlockSpec((tm, d), lambda i, e, *_: (i, 0)),
                pl.BlockSpec((tm, LANES), lambda i, e, *_: (i, 0)),
                pl.BlockSpec((tm, d), lambda i, e, *_: (i, 0)),
                pl.BlockSpec((1, d, MOE_FF), lambda i, e, *_: (e, 0, 0)),
                pl.BlockSpec((1, d, MOE_FF), lambda i, e, *_: (e, 0, 0)),
                pl.BlockSpec((1, MOE_FF, d), lambda i, e, *_: (e, 0, 0)),
                pl.BlockSpec((1, d), lambda i, e, *_: (0, 0))]
    out_specs = [pl.BlockSpec((tm, d), lambda i, e, *_: (i, 0))]
    out_shape = [jax.ShapeDtypeStruct((n, d), F32)]
    args = [h, gates, x, wg, wu, wd, g_final.reshape(1, d)]
    prefetch = []
    n_cache_pages = 0
    if cache is not None:
        cache_k, pt_flat, n_req, n_pages = cache
        n_cache_pages = n_pages // MOE_EXPERTS
        assert grid == (n_req, MOE_EXPERTS) and n_pages % MOE_EXPERTS == 0 and n_cache_pages % PAGES_PER_BLOCK == 0
        prefetch = [pt_flat]
        for r in range(n_cache_pages):
            in_specs.append(pl.BlockSpec(
                (1, PAGE_SIZE, ATT_HEADS, ATT_HEAD_DIM),
                lambda i, e, pt, r=r: (pt[i * n_pages + e * n_cache_pages + r], 0, 0, 0)))
        args += [cache_k] * n_cache_pages
        blocks_per_step = n_cache_pages // PAGES_PER_BLOCK
        out_specs.append(pl.BlockSpec((1, blocks_per_step, ATT_HEADS, ATT_HEAD_DIM), lambda i, e, *_: (i, e, 0, 0)))
        out_shape.append(jax.ShapeDtypeStruct((n_req, n_pages // PAGES_PER_BLOCK, ATT_HEADS, ATT_HEAD_DIM), F32))
    outs = pl.pallas_call(
        functools.partial(_moe_expert_kernel, final_norm=final_norm, n_cache_pages=n_cache_pages),
        grid_spec=pltpu.PrefetchScalarGridSpec(
            num_scalar_prefetch=len(prefetch), grid=grid, in_specs=in_specs, out_specs=out_specs),
        out_shape=out_shape,
        compiler_params=_params("parallel", "arbitrary"),
        name="moe_experts",
    )(*prefetch, *args)
    return outs if cache is not None else outs[0]


def _hier_moe(x, g_norm, w_cat, b_cat, wg, wu, wd, g_final, tm, final_norm, cache=None):
    h, gates = _moe_router(x, g_norm, w_cat, b_cat, tm)
    return _moe_experts(h, gates, x, wg, wu, wd, g_final, tm, final_norm, cache)


def _moba_prompt_kernel(q_ref, k_ref, v_ref, o_ref, *, t_len):
    nb = t_len // MOBA_BLOCK
    blk = MOBA_BLOCK
    scale = ATT_HEAD_DIM ** -0.5
    kf = k_ref[...]
    kb = kf.astype(BF16)
    vb = v_ref[...].astype(BF16)
    kmean = jnp.mean(kf.reshape(nb, blk, ATT_HEAD_DIM), axis=1)
    row = lax.broadcasted_iota(jnp.int32, (blk, blk), 0)
    col = lax.broadcasted_iota(jnp.int32, (blk, blk), 1)
    causal = col <= row
    for i in range(nb):
        qi = q_ref[i * blk:(i + 1) * blk, :]
        qi_b = qi.astype(BF16)
        sel = [None] * i
        if i > MOBA_TOPK:
            gate = lax.dot_general(qi, kmean, NT_DIMS, precision=HIGHEST, preferred_element_type=F32)
            g = [gate[:, j:j + 1] for j in range(i)]
            rank = [jnp.zeros((blk, 1), F32) for _ in range(i)]
            for lo_j in range(i):
                for hi_j in range(lo_j + 1, i):
                    lo_wins = jnp.where(g[lo_j] >= g[hi_j], 1.0, 0.0)
                    rank[hi_j] = rank[hi_j] + lo_wins
                    rank[lo_j] = rank[lo_j] + (1.0 - lo_wins)
            sel = [r < float(MOBA_TOPK) for r in rank]
        pieces = []
        for j in range(i + 1):
            s = lax.dot_general(qi_b, kb[j * blk:(j + 1) * blk], NT_DIMS, preferred_element_type=F32) * scale
            if j == i:
                s = jnp.where(causal, s, NEG_INF)
            elif sel[j] is not None:
                s = jnp.where(sel[j], s, NEG_INF)
            pieces.append(s)
        m = jnp.max(pieces[0], axis=-1, keepdims=True)
        for s in pieces[1:]:
            m = jnp.maximum(m, jnp.max(s, axis=-1, keepdims=True))
        l = jnp.zeros((blk, 1), F32)
        acc = jnp.zeros((blk, ATT_HEAD_DIM), F32)
        for j, s in enumerate(pieces):
            p = jnp.exp(s - m)
            l = l + jnp.sum(p, axis=-1, keepdims=True)
            acc = acc + jnp.dot(p.astype(BF16), vb[j * blk:(j + 1) * blk], preferred_element_type=F32)
        o_ref[i * blk:(i + 1) * blk, :] = acc / l


def _moba_prompt(q, k, v, n_seq, t_len):
    assert t_len % MOBA_BLOCK == 0
    spec = pl.BlockSpec((t_len, ATT_HEAD_DIM), lambda b, h: (b, h))
    return pl.pallas_call(
        functools.partial(_moba_prompt_kernel, t_len=t_len),
        grid=(n_seq, ATT_HEADS),
        in_specs=[spec, spec, spec],
        out_specs=spec,
        out_shape=jax.ShapeDtypeStruct(q.shape, F32),
        compiler_params=_params("parallel", "parallel"),
        name="moba_prompt",
    )(q, k, v)


def _top_blocks_kernel(q_ref, kmean_ref, idx_ref):
    q = q_ref[0]
    gate = jnp.sum(kmean_ref[0] * q[None, :, :], axis=-1, keepdims=True)
    n_blocks = gate.shape[0]
    blk_id = lax.broadcasted_iota(jnp.int32, gate.shape, 0)
    lane = lax.broadcasted_iota(jnp.int32, (ATT_HEADS, LANES), 1)
    out = jnp.zeros((ATT_HEADS, LANES), jnp.int32)
    for r in range(MOBA_TOPK):
        m = jnp.max(gate, axis=0, keepdims=True)
        idx = jnp.min(jnp.where(gate == m, blk_id, n_blocks), axis=0, keepdims=True)
        gate = jnp.where(blk_id == idx, -jnp.inf, gate)
        out = jnp.where(lane == r, idx[0], out)
    idx_ref[0] = out


def _top_blocks(q3, kmean):
    n_req, n_blocks = kmean.shape[:2]
    return pl.pallas_call(
        _top_blocks_kernel,
        grid=(n_req,),
        in_specs=[pl.BlockSpec((1, ATT_HEADS, ATT_HEAD_DIM), lambda b: (b, 0, 0)),
                  pl.BlockSpec((1, n_blocks, ATT_HEADS, ATT_HEAD_DIM), lambda b: (b, 0, 0, 0))],
        out_specs=pl.BlockSpec((1, ATT_HEADS, LANES), lambda b: (b, 0, 0)),
        out_shape=jax.ShapeDtypeStruct((n_req, ATT_HEADS, LANES), jnp.int32),
        compiler_params=_params("parallel"),
        name="moba_top_blocks",
    )(q3, kmean)


SEL_PAGES = MOBA_TOPK * PAGES_PER_BLOCK


def _moba_sample_kernel(idx_ref, pt_ref, q_ref, kn_ref, vn_ref, ck_ref, cv_ref, o_ref, kbuf, vbuf, sem, *, n_req, n_pages):
    b = pl.program_id(0)
    slot = b % 2

    def slab_copies(req, dst_slot):
        copies = []
        for h in range(ATT_HEADS):
            for sel in range(MOBA_TOPK):
                blk = idx_ref[(req * ATT_HEADS + h) * MOBA_TOPK + sel]
                for r in range(PAGES_PER_BLOCK):
                    page = pt_ref[req * n_pages + blk * PAGES_PER_BLOCK + r]
                    j = sel * PAGES_PER_BLOCK + r
                    copies.append(pltpu.make_async_copy(ck_ref.at[page, :, h, :], kbuf.at[dst_slot, h, j], sem.at[0, dst_slot]))
                    copies.append(pltpu.make_async_copy(cv_ref.at[page, :, h, :], vbuf.at[dst_slot, h, j], sem.at[1, dst_slot]))
        return copies

    @pl.when(b == 0)
    def _():
        for cp in slab_copies(0, 0):
            cp.start()

    @pl.when(b + 1 < n_req)
    def _():
        for cp in slab_copies(b + 1, 1 - slot):
            cp.start()

    for cp in slab_copies(b, slot):
        cp.wait()

    scale = ATT_HEAD_DIM ** -0.5
    q_all = q_ref[0]
    kn_all = kn_ref[0]
    vn_all = vn_ref[0]
    for h in range(ATT_HEADS):
        q = q_all[h:h + 1, :]
        q8 = jnp.broadcast_to(q, (SUBLANES, ATT_HEAD_DIM)).astype(BF16)
        s_own = jnp.sum(q * kn_all[h:h + 1, :], axis=-1, keepdims=True) * scale
        scores = [lax.dot_general(q8, kbuf[slot, h, j].astype(BF16), NT_DIMS, preferred_element_type=F32) * scale
                  for j in range(SEL_PAGES)]
        m = s_own
        for s in scores:
            m = jnp.maximum(m, jnp.max(s, axis=-1, keepdims=True))
        p_own = jnp.exp(s_own - m)
        l = p_own
        acc = p_own * vn_all[h:h + 1, :]
        for j, s in enumerate(scores):
            p = jnp.exp(s - m)
            l = l + jnp.sum(p, axis=-1, keepdims=True)
            acc = acc + jnp.dot(p.astype(BF16), vbuf[slot, h, j].astype(BF16), preferred_element_type=F32)
        o_ref[0, h:h + 1, :] = (acc / l)[0:1, :]


def _moba_sample(q3, k3, v3, cache_k, cache_v, idx_flat, page_table_flat, n_req, n_pages):
    row_spec = pl.BlockSpec((1, ATT_HEADS, ATT_HEAD_DIM), lambda b, idx, pt: (b, 0, 0))
    any_spec = pl.BlockSpec(memory_space=pl.ANY)
    slab_buf = pltpu.VMEM((2, ATT_HEADS, SEL_PAGES, PAGE_SIZE, ATT_HEAD_DIM), F32)
    return pl.pallas_call(
        functools.partial(_moba_sample_kernel, n_req=n_req, n_pages=n_pages),
        grid_spec=pltpu.PrefetchScalarGridSpec(
            num_scalar_prefetch=2,
            grid=(n_req,),
            in_specs=[row_spec, row_spec, row_spec, any_spec, any_spec],
            out_specs=row_spec,
            scratch_shapes=[slab_buf, slab_buf, pltpu.SemaphoreType.DMA((2, 2))],
        ),
        out_shape=jax.ShapeDtypeStruct(q3.shape, F32),
        compiler_params=_params("arbitrary"),
        name="moba_sample",
    )(idx_flat, page_table_flat, q3, k3, v3, cache_k, cache_v)


def _pair_block_diag(w):
    hds, d, _ = w.shape
    wp = w.reshape(hds // 2, 2, d, d)
    z = jnp.zeros((hds // 2, d, d), w.dtype)
    top = jnp.concatenate([wp[:, 0], z], axis=-1)
    bot = jnp.concatenate([z, wp[:, 1]], axis=-1)
    return jnp.concatenate([top, bot], axis=-2)


def _lane_pad(vec, offset):
    out = jnp.zeros((1, LANES), F32)
    return out.at[0, offset:offset + vec.shape[0]].set(vec.astype(F32))


def kernel(x_prompt, x_sample, state_conv, state_rglru_h, state_gdn, cache_k, cache_v, page_table, norm_mix, norm_ffn, norm_final, w_in0, conv0_w, conv0_b, rg_wa, rg_ba, rg_wi, rg_bi, rg_lambda, gdn_a_log, gdn_dt_bias, gdn_norm, w_out0, w_qkv1, w_out1, moe_w_group, moe_b_group, moe_w_router, moe_b_router, moe_w_gate, moe_w_up, moe_w_down):
    bp, tp, d = x_prompt.shape
    bs, ts, _ = x_sample.shape
    n_pages = page_table.shape[1]
    assert ts == 1 and d == D_MODEL
    assert n_pages % PAGES_PER_BLOCK == 0
    assert n_pages // PAGES_PER_BLOCK >= MOBA_TOPK

    w_in0_b = jnp.pad(w_in0, ((0, 0), (0, IN0_PAD - IN0_DIM))).astype(BF16)
    w_out0_b = w_out0.astype(BF16)
    w_qkv1_b = w_qkv1.astype(BF16)
    w_out1_b = w_out1.astype(BF16)
    wg_b, wu_b, wd_b = moe_w_gate.astype(BF16), moe_w_up.astype(BF16), moe_w_down.astype(BF16)
    mix_w = (conv0_w, conv0_b.reshape(1, CONV_CH),
             _pair_block_diag(rg_wa).astype(BF16), rg_ba.reshape(1, RG_WIDTH),
             _pair_block_diag(rg_wi).astype(BF16), rg_bi.reshape(1, RG_WIDTH),
             rg_lambda.reshape(1, RG_WIDTH), _lane_pad(gdn_a_log, GDN_HEADS), _lane_pad(gdn_dt_bias, GDN_HEADS),
             gdn_norm.reshape(1, GDN_DV), w_out0_b)
    router_w = [jnp.pad(jnp.concatenate([moe_w_router[l], moe_w_group[l]], axis=-1),
                        ((0, 0), (0, LANES - MOE_EXPERTS - MOE_GROUPS))) for l in range(2)]
    router_b = [_lane_pad(jnp.concatenate([moe_b_router[l], moe_b_group[l]]), 0) for l in range(2)]

    xp = x_prompt.reshape(bp * tp, d)
    xs8 = jnp.pad(x_sample, ((0, 0), (0, SUBLANES - ts), (0, 0))).reshape(bs * SUBLANES, d)

    (proj_p,) = _norm_matmul(xp, norm_mix[0], w_in0_b, (IN0_PAD,), tm=512)
    (proj_s,) = _norm_matmul(xs8, norm_mix[0], w_in0_b, (IN0_PAD,), tm=bs * SUBLANES)
    xp, p_h, p_gdn = _mixer(proj_p, xp,
                            jnp.zeros((bp, SUBLANES, CONV_CH), F32), jnp.zeros((bp, 1, RG_WIDTH), F32),
                            jnp.zeros((bp, GDN_HEADS, GDN_DK, GDN_DV), F32), mix_w,
                            n_seq=bp, t_len=tp, tc=256, chunk=GDN_CHUNK, t_valid=tp)
    cbuf_s = jnp.pad(state_conv, ((0, 0), (SUBLANES - (CONV_W - 1), 0), (0, 0)))
    xs8, s_h, s_gdn = _mixer(proj_s, xs8, cbuf_s, state_rglru_h.reshape(bs, 1, RG_WIDTH), state_gdn, mix_w,
                             n_seq=bs, t_len=SUBLANES, tc=SUBLANES, chunk=SUBLANES, t_valid=ts)
    xs = xs8.reshape(bs, SUBLANES, d)[:, 0]
    p_conv = proj_p.reshape(bp, tp, IN0_PAD)[:, tp - (CONV_W - 1):, :CONV_CH]
    s_conv = jnp.concatenate([state_conv[:, ts:], proj_s.reshape(bs, SUBLANES, IN0_PAD)[:, :ts, :CONV_CH]], axis=1)

    moe0 = (norm_ffn[0], router_w[0], router_b[0], wg_b[0], wu_b[0], wd_b[0], norm_final)
    pt_flat = page_table.reshape(-1)
    xp, kmean = _hier_moe(xp, *moe0, tm=(bp * tp) // bs, final_norm=False, cache=(cache_k, pt_flat, bs, n_pages))
    xs = _hier_moe(xs, *moe0, tm=bs, final_norm=False)

    hd_all = ATT_HEADS * ATT_HEAD_DIM
    qp, kp, vp = _norm_matmul(xp, norm_mix[1], w_qkv1_b, (hd_all,) * 3, tm=512)
    op = _moba_prompt(qp, kp, vp, bp, tp)
    xp = _matmul_residual(op, w_out1_b, xp, tm=512)

    qs, ks, vs = _norm_matmul(xs, norm_mix[1], w_qkv1_b, (hd_all,) * 3, tm=bs)
    idx = _top_blocks(qs.reshape(bs, ATT_HEADS, ATT_HEAD_DIM), kmean)
    row3 = lambda a: a.reshape(bs, ATT_HEADS, ATT_HEAD_DIM)
    os_ = _moba_sample(row3(qs), row3(ks), row3(vs), cache_k, cache_v,
                       idx[:, :, :MOBA_TOPK].reshape(-1), pt_flat, bs, n_pages)
    xs = _matmul_residual(os_.reshape(bs, hd_all), w_out1_b, xs, tm=bs)

    moe1 = (norm_ffn[1], router_w[1], router_b[1], wg_b[1], wu_b[1], wd_b[1], norm_final)
    y_p = _hier_moe(xp, *moe1, tm=512, final_norm=True)
    y_s = _hier_moe(xs, *moe1, tm=bs, final_norm=True)

    heads = lambda a, b_, t_: a.reshape(b_, t_, ATT_HEADS, ATT_HEAD_DIM)
    return (y_p.reshape(bp, tp, d), y_s.reshape(bs, ts, d),
            p_conv, p_h.reshape(bp, RG_WIDTH), p_gdn,
            heads(kp, bp, tp), heads(vp, bp, tp),
            s_conv, s_h.reshape(bs, RG_WIDTH), s_gdn,
            heads(ks, bs, ts), heads(vs, bs, ts))
```

```python
import functools

import jax
import jax.numpy as jnp
from jax import lax
from jax.experimental import pallas as pl
from jax.experimental.pallas import tpu as pltpu

F32 = jnp.float32
BF16 = jnp.bfloat16
HIGHEST = lax.Precision.HIGHEST

D_MODEL = 1024
RG_WIDTH = 512
RG_HEADS = 8
RG_HEAD_DIM = 64
RG_C = 8.0
GDN_HEADS = 4
GDN_DK = 128
GDN_DV = 128
GDN_CHUNK = 64
CONV_W = 4
CONV_CH = 2048
IN0_DIM = 3080
IN0_PAD = 3200
ATT_HEADS = 8
ATT_HEAD_DIM = 128
MOBA_BLOCK = 256
MOBA_TOPK = 3
PAGE_SIZE = 128
PAGES_PER_BLOCK = MOBA_BLOCK // PAGE_SIZE
MOE_GROUPS = 4
MOE_EXPERTS_PER_GROUP = 4
MOE_EXPERTS = 16
MOE_FF = 512
RMS_EPS = 1e-6
L2_EPS = 1e-6
NEG_INF = -1e30
LOG2_E = 1.4426950408889634

LANES = 128
SUBLANES = 8
VMEM_LIMIT = 56 * 1024 * 1024
MOE_TM = 1024
NT_DIMS = (((1,), (1,)), ((), ()))
TN_DIMS = (((0,), (0,)), ((), ()))


def _params(*sem):
    return pltpu.CompilerParams(dimension_semantics=sem, vmem_limit_bytes=VMEM_LIMIT)


def _rms(x, g):
    return x * lax.rsqrt(jnp.mean(x * x, axis=-1, keepdims=True) + RMS_EPS) * g


def _softplus(x):
    return jnp.maximum(x, 0.0) + jnp.log1p(jnp.exp(-jnp.abs(x)))


def _sigmoid(x):
    return 1.0 / (1.0 + jnp.exp(-x))


def _norm_matmul_kernel(x_ref, g_ref, w_ref, *out_refs, widths):
    h = _rms(x_ref[...], g_ref[...]).astype(BF16)
    lo = 0
    for o_ref, width in zip(out_refs, widths):
        o_ref[...] = jnp.dot(h, w_ref[:, lo:lo + width], preferred_element_type=F32)
        lo += width


def _norm_matmul(x, g, w_bf16, widths, tm):
    n, d = x.shape
    assert n % tm == 0 and sum(widths) == w_bf16.shape[1]
    return pl.pallas_call(
        functools.partial(_norm_matmul_kernel, widths=widths),
        grid=(n // tm,),
        in_specs=[pl.BlockSpec((tm, d), lambda i: (i, 0)),
                  pl.BlockSpec((1, d), lambda i: (0, 0)),
                  pl.BlockSpec(w_bf16.shape, lambda i: (0, 0))],
        out_specs=[pl.BlockSpec((tm, wd), lambda i: (i, 0)) for wd in widths],
        out_shape=[jax.ShapeDtypeStruct((n, wd), F32) for wd in widths],
        compiler_params=_params("parallel"),
        name="norm_matmul",
    )(x, g.reshape(1, d), w_bf16)


def _matmul_residual_kernel(a_ref, w_ref, r_ref, o_ref):
    o_ref[...] = r_ref[...] + jnp.dot(a_ref[...].astype(BF16), w_ref[...], preferred_element_type=F32)


def _matmul_residual(a, w_bf16, res, tm):
    n, k = a.shape
    m = w_bf16.shape[1]
    assert n % tm == 0
    return pl.pallas_call(
        _matmul_residual_kernel,
        grid=(n // tm,),
        in_specs=[pl.BlockSpec((tm, k), lambda i: (i, 0)),
                  pl.BlockSpec((k, m), lambda i: (0, 0)),
                  pl.BlockSpec((tm, m), lambda i: (i, 0))],
        out_specs=pl.BlockSpec((tm, m), lambda i: (i, 0)),
        out_shape=jax.ShapeDtypeStruct((n, m), F32),
        compiler_params=_params("parallel"),
        name="matmul_residual",
    )(a, w_bf16, res)


def _linear_scan(a, u, n_rows):
    row = lax.broadcasted_iota(jnp.int32, a.shape, 0)
    d = 1
    while d < n_rows:
        a_prev = jnp.where(row >= d, pltpu.roll(a, d, 0), 1.0)
        u_prev = jnp.where(row >= d, pltpu.roll(u, d, 0), 0.0)
        u = a * u_prev + u
        a = a * a_prev
        d *= 2
    return a, u


SPLIT_POWER_MAX = 2


def _bmm(a, b):
    return jnp.einsum('bij,bjk->bik', a.astype(BF16), b.astype(BF16), preferred_element_type=F32)


def _split_bf16(x):
    hi = x.astype(BF16)
    return hi, x - hi.astype(F32)


def _bmm_split(a, b):
    a_hi, a_lo = _split_bf16(a)
    b_hi, b_lo = _split_bf16(b)
    return _bmm(a_hi, b_hi) + (_bmm(a_hi, b_lo) + _bmm(a_lo, b_hi))


def _unit_lower_solve(lmat, rhs, c):
    m = -lmat
    sol = rhs + _bmm_split(m, rhs)
    p = 2
    while p < c:
        mm = _bmm_split if p <= SPLIT_POWER_MAX else _bmm
        m = mm(m, m)
        sol = sol + mm(m, sol)
        p *= 2
    return sol


def _mixer_kernel(conv_ref, gate_ref, z_ref, ba_ref, x_ref, cbuf0_ref, h0_ref, s0_ref,
                  cw_ref, cb_ref, wa_ref, rba_ref, wi_ref, rbi_ref, lam_ref, alog_ref, dtb_ref, gnorm_ref, wout_ref,
                  x1_ref, ht_ref, st_ref,
                  cbuf, hcar, state, act, *, tc, chunk, t_valid, nt):
    t = pl.program_id(1)

    @pl.when(t == 0)
    def _():
        cbuf[0:SUBLANES, :] = cbuf0_ref[0]
        hcar[...] = h0_ref[0]
        state[...] = s0_ref[0]

    cbuf[SUBLANES:SUBLANES + tc, :] = conv_ref[...]
    base = SUBLANES - (CONV_W - 1)
    y = cb_ref[...] + cbuf[base:base + tc, :] * cw_ref[0:1, :]
    for j in range(1, CONV_W):
        y = y + cbuf[base + j:base + j + tc, :] * cw_ref[j:j + 1, :]
    cbuf[0:SUBLANES, :] = cbuf[tc:tc + SUBLANES, :]

    masked = t_valid < nt * tc
    if masked:
        valid = (t * tc + lax.broadcasted_iota(jnp.int32, (tc, 1), 0)) < t_valid

    xr = y[:, :RG_WIDTH]
    xr_b = xr.astype(BF16)
    pair = 2 * RG_HEAD_DIM
    r_lin = jnp.concatenate(
        [jnp.dot(xr_b[:, pair * i:pair * (i + 1)], wa_ref[i], preferred_element_type=F32) for i in range(RG_WIDTH // pair)], axis=-1)
    i_lin = jnp.concatenate(
        [jnp.dot(xr_b[:, pair * i:pair * (i + 1)], wi_ref[i], preferred_element_type=F32) for i in range(RG_WIDTH // pair)], axis=-1)
    r = _sigmoid(r_lin + rba_ref[...])
    gi = _sigmoid(i_lin + rbi_ref[...])
    log_a = -RG_C * r * _softplus(-lam_ref[...])
    a = jnp.exp(log_a)
    u = jnp.sqrt(-jnp.tanh(log_a) * (a * a + 1.0)) * (gi * xr)
    if masked:
        a = jnp.where(valid, a, 1.0)
        u = jnp.where(valid, u, 0.0)
    a_cum, h_loc = _linear_scan(a, u, tc)
    h = h_loc + a_cum * hcar[...]
    hcar[...] = h[tc - 1:tc, :]
    act[:, 0:RG_WIDTH] = (h * jax.nn.gelu(gate_ref[...])).astype(BF16)

    qkv = y[:, RG_WIDTH:]
    qkv = qkv * _sigmoid(qkv)
    ba = ba_ref[...]
    beta_all = _sigmoid(ba)
    g_all = -jnp.exp(alog_ref[...]) * _softplus(ba + dtb_ref[...])
    if masked:
        beta_all = jnp.where(valid, beta_all, 0.0)
        g_all = jnp.where(valid, g_all, 0.0)
    in_chunk = lax.broadcasted_iota(jnp.int32, g_all.shape, 0) & (chunk - 1)
    gcum_all = g_all
    d = 1
    while d < chunk:
        gcum_all = gcum_all + jnp.where(in_chunk >= d, pltpu.roll(gcum_all, d, 0), 0.0)
        d *= 2

    nc = tc // chunk

    def chunk_head_batch(cols_of_head):
        per_head = [cols_of_head(hd) for hd in range(GDN_HEADS)]
        return jnp.concatenate([per_head[hd][ci * chunk:(ci + 1) * chunk][None]
                                for ci in range(nc) for hd in range(GDN_HEADS)], axis=0)

    def l2_normalised(x):
        return x * lax.rsqrt(jnp.sum(x * x, axis=-1, keepdims=True) + L2_EPS)

    q_b = chunk_head_batch(lambda hd: l2_normalised(qkv[:, GDN_DK * hd:GDN_DK * (hd + 1)]) * (GDN_DK ** -0.5))
    k_b = chunk_head_batch(lambda hd: l2_normalised(qkv[:, 512 + GDN_DK * hd:512 + GDN_DK * (hd + 1)]))
    v_b = chunk_head_batch(lambda hd: qkv[:, 1024 + GDN_DV * hd:1024 + GDN_DV * (hd + 1)])
    beta_b = chunk_head_batch(lambda hd: beta_all[:, hd:hd + 1])
    gcol = chunk_head_batch(lambda hd: gcum_all[:, GDN_HEADS + hd:GDN_HEADS + hd + 1])

    nbatch = nc * GDN_HEADS
    crow = lax.broadcasted_iota(jnp.int32, (nbatch, chunk, chunk), 1)
    ccol = lax.broadcasted_iota(jnp.int32, (nbatch, chunk, chunk), 2)
    lower = crow >= ccol
    strict = crow > ccol
    grow = jnp.sum(jnp.where(crow == ccol, gcol, 0.0), axis=1, keepdims=True)
    decay = jnp.where(lower, jnp.exp(jnp.where(lower, gcol - grow, 0.0)), 0.0)
    kb = k_b * beta_b
    k_bf = k_b.astype(BF16)
    kk = jnp.einsum('bik,bjk->bij', kb.astype(BF16), k_bf, preferred_element_type=F32)
    rhs = jnp.concatenate([v_b * beta_b, kb * jnp.exp(gcol)], axis=-1)
    sol = _unit_lower_solve(jnp.where(strict, kk * decay, 0.0), rhs, chunk)
    qk = jnp.where(lower, jnp.einsum('bik,bjk->bij', q_b.astype(BF16), k_bf, preferred_element_type=F32) * decay, 0.0)
    qk_bf = qk.astype(BF16)
    q_dec = (q_b * jnp.exp(gcol)).astype(BF16)
    g_last = gcol[:, chunk - 1:chunk, :]
    k_dec = (k_b * jnp.exp(g_last - gcol)).astype(BF16)
    s_decay = jnp.exp(g_last)
    u_all = sol[:, :, :GDN_DV]
    w_bf = sol[:, :, GDN_DV:].astype(BF16)

    gnorm = gnorm_ref[...]
    z_all = z_ref[...]
    s_heads = [state[hd] for hd in range(GDN_HEADS)]
    for ci in range(nc):
        lo = ci * chunk
        for hd in range(GDN_HEADS):
            bi = ci * GDN_HEADS + hd
            s_h = s_heads[hd]
            s_bf = s_h.astype(BF16)
            v_new = u_all[bi] - jnp.dot(w_bf[bi], s_bf, preferred_element_type=F32)
            v_new_bf = v_new.astype(BF16)
            o = (jnp.dot(q_dec[bi], s_bf, preferred_element_type=F32)
                 + jnp.dot(qk_bf[bi], v_new_bf, preferred_element_type=F32))
            s_heads[hd] = s_h * s_decay[bi] + lax.dot_general(k_dec[bi], v_new_bf, TN_DIMS, preferred_element_type=F32)
            zh = z_all[lo:lo + chunk, GDN_DV * hd:GDN_DV * (hd + 1)]
            o = _rms(o, gnorm) * (zh * _sigmoid(zh))
            act[lo:lo + chunk, RG_WIDTH + GDN_DV * hd:RG_WIDTH + GDN_DV * (hd + 1)] = o.astype(BF16)
    for hd in range(GDN_HEADS):
        state[hd] = s_heads[hd]

    x1_ref[...] = x_ref[...] + jnp.dot(act[...], wout_ref[...], preferred_element_type=F32)

    @pl.when(t == nt - 1)
    def _():
        ht_ref[0] = hcar[...]
        st_ref[0] = state[...]


def _mixer(proj, x, cbuf0, h0, s0, wts, *, n_seq, t_len, tc, chunk, t_valid):
    nt = t_len // tc
    assert t_len % tc == 0 and tc % chunk == 0 and tc % SUBLANES == 0
    row = lambda b, t: b * nt + t
    const2 = lambda b, t: (0, 0)
    (cw, cb, wa4, rba, wi4, rbi, lam, alog, dtb, gnorm, wout) = wts
    in_specs = [
        pl.BlockSpec((tc, CONV_CH), lambda b, t: (row(b, t), 0)),
        pl.BlockSpec((tc, RG_WIDTH), lambda b, t: (row(b, t), CONV_CH // RG_WIDTH)),
        pl.BlockSpec((tc, RG_WIDTH), lambda b, t: (row(b, t), CONV_CH // RG_WIDTH + 1)),
        pl.BlockSpec((tc, LANES), lambda b, t: (row(b, t), (CONV_CH + 2 * RG_WIDTH) // LANES)),
        pl.BlockSpec((tc, D_MODEL), lambda b, t: (row(b, t), 0)),
        pl.BlockSpec((1, SUBLANES, CONV_CH), lambda b, t: (b, 0, 0)),
        pl.BlockSpec((1, 1, RG_WIDTH), lambda b, t: (b, 0, 0)),
        pl.BlockSpec((1, GDN_HEADS, GDN_DK, GDN_DV), lambda b, t: (b, 0, 0, 0)),
        pl.BlockSpec(cw.shape, const2), pl.BlockSpec(cb.shape, const2),
        pl.BlockSpec(wa4.shape, lambda b, t: (0, 0, 0)), pl.BlockSpec(rba.shape, const2),
        pl.BlockSpec(wi4.shape, lambda b, t: (0, 0, 0)), pl.BlockSpec(rbi.shape, const2),
        pl.BlockSpec(lam.shape, const2), pl.BlockSpec(alog.shape, const2), pl.BlockSpec(dtb.shape, const2),
        pl.BlockSpec(gnorm.shape, const2), pl.BlockSpec(wout.shape, const2),
    ]
    out_specs = [
        pl.BlockSpec((tc, D_MODEL), lambda b, t: (row(b, t), 0)),
        pl.BlockSpec((1, 1, RG_WIDTH), lambda b, t: (b, 0, 0)),
        pl.BlockSpec((1, GDN_HEADS, GDN_DK, GDN_DV), lambda b, t: (b, 0, 0, 0)),
    ]
    out_shape = [
        jax.ShapeDtypeStruct((n_seq * t_len, D_MODEL), F32),
        jax.ShapeDtypeStruct((n_seq, 1, RG_WIDTH), F32),
        jax.ShapeDtypeStruct((n_seq, GDN_HEADS, GDN_DK, GDN_DV), F32),
    ]
    return pl.pallas_call(
        functools.partial(_mixer_kernel, tc=tc, chunk=chunk, t_valid=t_valid, nt=nt),
        grid=(n_seq, nt),
        in_specs=in_specs, out_specs=out_specs, out_shape=out_shape,
        scratch_shapes=[pltpu.VMEM((tc + SUBLANES, CONV_CH), F32),
                        pltpu.VMEM((1, RG_WIDTH), F32),
                        pltpu.VMEM((GDN_HEADS, GDN_DK, GDN_DV), F32),
                        pltpu.VMEM((tc, D_MODEL), BF16)],
        compiler_params=_params("arbitrary", "arbitrary"),
        name="mixer0",
    )(proj, proj, proj, proj, x, cbuf0, h0, s0, cw, cb, wa4, rba, wi4, rbi, lam, alog, dtb, gnorm, wout)


def _router_kernel(x_ref, g_ref, w_ref, b_ref, h_ref, gates_ref):
    h = _rms(x_ref[...], g_ref[...])
    h_ref[...] = h.astype(BF16)
    logits = jnp.dot(h, w_ref[...], precision=HIGHEST, preferred_element_type=F32) + b_ref[...]
    lane = lax.broadcasted_iota(jnp.int32, logits.shape, 1)
    big = jnp.int32(1 << 20)
    is_grp = (lane >= MOE_EXPERTS) & (lane < MOE_EXPERTS + MOE_GROUPS)
    gl = jnp.where(is_grp, logits, -jnp.inf)
    gmax = jnp.max(gl, axis=-1, keepdims=True)
    gidx = jnp.min(jnp.where(gl == gmax, lane, big), axis=-1, keepdims=True) - MOE_EXPERTS
    g_top = 1.0 / jnp.sum(jnp.exp(gl - gmax), axis=-1, keepdims=True)
    in_grp = (lane >= gidx * MOE_EXPERTS_PER_GROUP) & (lane < (gidx + 1) * MOE_EXPERTS_PER_GROUP)
    el = jnp.where(in_grp, logits, -jnp.inf)
    m1 = jnp.max(el, axis=-1, keepdims=True)
    i1 = jnp.min(jnp.where(el == m1, lane, big), axis=-1, keepdims=True)
    el2 = jnp.where(lane == i1, -jnp.inf, el)
    m2 = jnp.max(el2, axis=-1, keepdims=True)
    i2 = jnp.min(jnp.where(el2 == m2, lane, big), axis=-1, keepdims=True)
    e2 = jnp.exp(m2 - m1)
    w1 = g_top / (1.0 + e2)
    w2 = g_top * e2 / (1.0 + e2)
    gates_ref[...] = jnp.where(lane == i1, w1, 0.0) + jnp.where(lane == i2, w2, 0.0)


def _moe_router(x, g, w_cat, b_cat, tm):
    n, d = x.shape
    return pl.pallas_call(
        _router_kernel,
        grid=(n // tm,),
        in_specs=[pl.BlockSpec((tm, d), lambda i: (i, 0)),
                  pl.BlockSpec((1, d), lambda i: (0, 0)),
                  pl.BlockSpec((d, LANES), lambda i: (0, 0)),
                  pl.BlockSpec((1, LANES), lambda i: (0, 0))],
        out_specs=[pl.BlockSpec((tm, d), lambda i: (i, 0)),
                   pl.BlockSpec((tm, LANES), lambda i: (i, 0))],
        out_shape=[jax.ShapeDtypeStruct((n, d), BF16), jax.ShapeDtypeStruct((n, LANES), F32)],
        compiler_params=_params("parallel"),
        name="moe_router",
    )(x, g.reshape(1, d), w_cat, b_cat)


def _block_means(page_refs, o_ref):
    for b in range(len(page_refs) // PAGES_PER_BLOCK):
        s = jnp.sum(page_refs[PAGES_PER_BLOCK * b][0], axis=0)
        for r in range(1, PAGES_PER_BLOCK):
            s = s + jnp.sum(page_refs[PAGES_PER_BLOCK * b + r][0], axis=0)
        o_ref[0, b] = s / MOBA_BLOCK


def _moe_expert_kernel(*refs, final_norm, n_cache_pages):
    if n_cache_pages:
        refs = refs[1:]
    h_ref, gates_ref, x_ref, wg_ref, wu_ref, wd_ref, gf_ref = refs[:7]
    page_refs = refs[7:7 + n_cache_pages]
    o_ref = refs[7 + n_cache_pages]
    e = pl.program_id(1)

    if n_cache_pages:
        _block_means(page_refs, refs[8 + n_cache_pages])

    @pl.when(e == 0)
    def _():
        o_ref[...] = x_ref[...]

    h = h_ref[...]
    a = jnp.dot(h, wg_ref[0], preferred_element_type=F32)
    b = jnp.dot(h, wu_ref[0], preferred_element_type=F32)
    gates = gates_ref[...]
    lane = lax.broadcasted_iota(jnp.int32, gates.shape, 1)
    gcol = jnp.sum(jnp.where(lane == e, gates, 0.0), axis=-1, keepdims=True)
    hid = (a * _sigmoid(a)) * b * gcol
    o_ref[...] += jnp.dot(hid.astype(BF16), wd_ref[0], preferred_element_type=F32)

    if final_norm:
        @pl.when(e == MOE_EXPERTS - 1)
        def _():
            o_ref[...] = _rms(o_ref[...], gf_ref[...])


def _moe_experts(h, gates, x, wg, wu, wd, g_final, tm, final_norm, cache=None):
    n, d = x.shape
    grid = (n // tm, MOE_EXPERTS)
    in_specs = [pl.BlockSpec((tm, d), lambda i, e, *_: (i, 0)),
                pl.BlockSpec((tm, LANES), lambda i, e, *_: (i, 0)),
                pl.BlockSpec((tm, d), lambda i, e, *_: (i, 0)),
                pl.BlockSpec((1, d, MOE_FF), lambda i, e, *_: (e, 0, 0)),
                pl.BlockSpec((1, d, MOE_FF), lambda i, e, *_: (e, 0, 0)),
                pl.BlockSpec((1, MOE_FF, d), lambda i, e, *_: (e, 0, 0)),
                pl.BlockSpec((1, d), lambda i, e, *_: (0, 0))]
    out_specs = [pl.BlockSpec((tm, d), lambda i, e, *_: (i, 0))]
    out_shape = [jax.ShapeDtypeStruct((n, d), F32)]
    args = [h, gates, x, wg, wu, wd, g_final.reshape(1, d)]
    prefetch = []
    n_cache_pages = 0
    if cache is not None:
        cache_k, pt_flat, n_req, n_pages, first_page, span = cache
        steps = grid[0] * grid[1]
        assert (n_req * span) % steps == 0 and steps % n_req == 0
        n_cache_pages = n_req * span // steps
        groups = steps // n_req
        assert n_cache_pages % PAGES_PER_BLOCK == 0 and first_page % PAGES_PER_BLOCK == 0
        prefetch = [pt_flat]

        def page_index(i, e, pt, r):
            step = i * MOE_EXPERTS + e
            return (pt[(step // groups) * n_pages + first_page + (step % groups) * n_cache_pages + r], 0, 0, 0)

        for r in range(n_cache_pages):
            in_specs.append(pl.BlockSpec((1, PAGE_SIZE, ATT_HEADS, ATT_HEAD_DIM), functools.partial(page_index, r=r)))
        args += [cache_k] * n_cache_pages
        blocks_per_step = n_cache_pages // PAGES_PER_BLOCK
        out_specs.append(pl.BlockSpec(
            (1, blocks_per_step, ATT_HEADS, ATT_HEAD_DIM),
            lambda i, e, *_: ((i * MOE_EXPERTS + e) // groups, (i * MOE_EXPERTS + e) % groups, 0, 0)))
        out_shape.append(jax.ShapeDtypeStruct((n_req, span // PAGES_PER_BLOCK, ATT_HEADS, ATT_HEAD_DIM), F32))
    outs = pl.pallas_call(
        functools.partial(_moe_expert_kernel, final_norm=final_norm, n_cache_pages=n_cache_pages),
        grid_spec=pltpu.PrefetchScalarGridSpec(
            num_scalar_prefetch=len(prefetch), grid=grid, in_specs=in_specs, out_specs=out_specs),
        out_shape=out_shape,
        compiler_params=_params("parallel", "arbitrary"),
        name="moe_experts",
    )(*prefetch, *args)
    return outs if cache is not None else outs[0]


def _hier_moe(x, g_norm, w_cat, b_cat, wg, wu, wd, g_final, tm, final_norm, cache=None):
    h, gates = _moe_router(x, g_norm, w_cat, b_cat, tm)
    return _moe_experts(h, gates, x, wg, wu, wd, g_final, tm, final_norm, cache)


def _moba_prompt_kernel(q_ref, k_ref, v_ref, o_ref, *, t_len):
    nb = t_len // MOBA_BLOCK
    blk = MOBA_BLOCK
    c = (ATT_HEAD_DIM ** -0.5) * LOG2_E
    kf = k_ref[...]
    k_bf = kf.astype(BF16)
    q_t = q_ref[...].T
    q_t_bf = (q_t * c).astype(BF16)
    v_t_bf = v_ref[...].T.astype(BF16)
    kmean = jnp.mean(kf.reshape(nb, blk, ATT_HEAD_DIM), axis=1)
    key_pos = lax.broadcasted_iota(jnp.int32, (blk, blk), 0)
    query_pos = lax.broadcasted_iota(jnp.int32, (blk, blk), 1)
    causal = key_pos <= query_pos
    for i in range(nb):
        qs = slice(i * blk, (i + 1) * blk)
        sel = [None] * i
        if i > MOBA_TOPK:
            gate = jnp.dot(kmean, q_t[:, qs], precision=HIGHEST, preferred_element_type=F32)
            g = [gate[j:j + 1, :] for j in range(i)]
            rank = [jnp.zeros((1, blk), F32) for _ in range(i)]
            for lo_j in range(i):
                for hi_j in range(lo_j + 1, i):
                    lo_wins = jnp.where(g[lo_j] >= g[hi_j], 1.0, 0.0)
                    rank[hi_j] = rank[hi_j] + lo_wins
                    rank[lo_j] = rank[lo_j] + (1.0 - lo_wins)
            sel = [r < float(MOBA_TOPK) for r in rank]
        pieces = []
        s_max = None
        for j in range(i + 1):
            s = jnp.dot(k_bf[j * blk:(j + 1) * blk], q_t_bf[:, qs], preferred_element_type=F32)
            if j == i:
                s = jnp.where(causal, s, NEG_INF)
            elif sel[j] is not None:
                s = jnp.where(sel[j], s, NEG_INF)
            pieces.append(s)
            s_max = s if s_max is None else jnp.maximum(s_max, s)
        m = jnp.max(s_max, axis=0, keepdims=True)
        p_sum = jnp.zeros((blk, blk), F32)
        acc = jnp.zeros((ATT_HEAD_DIM, blk), F32)
        for j, s in enumerate(pieces):
            p = jnp.exp2(s - m)
            p_sum = p_sum + p
            acc = acc + jnp.dot(v_t_bf[:, j * blk:(j + 1) * blk], p.astype(BF16), preferred_element_type=F32)
        o_ref[qs, :] = (acc / jnp.sum(p_sum, axis=0, keepdims=True)).T


def _moba_prompt(q, k, v, n_seq, t_len):
    assert t_len % MOBA_BLOCK == 0
    spec = pl.BlockSpec((t_len, ATT_HEAD_DIM), lambda b, h: (b, h))
    return pl.pallas_call(
        functools.partial(_moba_prompt_kernel, t_len=t_len),
        grid=(n_seq, ATT_HEADS),
        in_specs=[spec, spec, spec],
        out_specs=spec,
        out_shape=jax.ShapeDtypeStruct(q.shape, F32),
        compiler_params=_params("parallel", "parallel"),
        name="moba_prompt",
    )(q, k, v)


def _top_blocks_kernel(q_ref, *refs):
    kmean_refs, idx_ref = refs[:-1], refs[-1]
    q = q_ref[0]
    kmean = jnp.concatenate([r[0] for r in kmean_refs], axis=0)
    gate = jnp.sum(kmean * q[None, :, :], axis=-1, keepdims=True)
    n_blocks = gate.shape[0]
    blk_id = lax.broadcasted_iota(jnp.int32, gate.shape, 0)
    lane = lax.broadcasted_iota(jnp.int32, (ATT_HEADS, LANES), 1)
    out = jnp.zeros((ATT_HEADS, LANES), jnp.int32)
    for r in range(MOBA_TOPK):
        m = jnp.max(gate, axis=0, keepdims=True)
        idx = jnp.min(jnp.where(gate == m, blk_id, n_blocks), axis=0, keepdims=True)
        gate = jnp.where(blk_id == idx, -jnp.inf, gate)
        out = jnp.where(lane == r, idx[0], out)
    idx_ref[0] = out


def _top_blocks(q3, kmean_parts):
    n_req = q3.shape[0]
    return pl.pallas_call(
        _top_blocks_kernel,
        grid=(n_req,),
        in_specs=[pl.BlockSpec((1, ATT_HEADS, ATT_HEAD_DIM), lambda b: (b, 0, 0))]
        + [pl.BlockSpec((1,) + part.shape[1:], lambda b: (b, 0, 0, 0)) for part in kmean_parts],
        out_specs=pl.BlockSpec((1, ATT_HEADS, LANES), lambda b: (b, 0, 0)),
        out_shape=jax.ShapeDtypeStruct((n_req, ATT_HEADS, LANES), jnp.int32),
        compiler_params=_params("parallel"),
        name="moba_top_blocks",
    )(q3, *kmean_parts)


SEL_PAGES = MOBA_TOPK * PAGES_PER_BLOCK


def _moba_sample_kernel(idx_ref, pt_ref, q_ref, kn_ref, vn_ref, ck_ref, cv_ref, o_ref, kbuf, vbuf, sem, *, n_req, n_pages):
    b = pl.program_id(0)
    slot = b % 2

    def slab_copies(req, dst_slot):
        copies = []
        for h in range(ATT_HEADS):
            for sel in range(MOBA_TOPK):
                blk = idx_ref[(req * ATT_HEADS + h) * MOBA_TOPK + sel]
                for r in range(PAGES_PER_BLOCK):
                    page = pt_ref[req * n_pages + blk * PAGES_PER_BLOCK + r]
                    j = sel * PAGES_PER_BLOCK + r
                    copies.append(pltpu.make_async_copy(ck_ref.at[page, :, h, :], kbuf.at[dst_slot, h, j], sem.at[0, dst_slot]))
                    copies.append(pltpu.make_async_copy(cv_ref.at[page, :, h, :], vbuf.at[dst_slot, h, j], sem.at[1, dst_slot]))
        return copies

    @pl.when(b == 0)
    def _():
        for cp in slab_copies(0, 0):
            cp.start()

    @pl.when(b + 1 < n_req)
    def _():
        for cp in slab_copies(b + 1, 1 - slot):
            cp.start()

    for cp in slab_copies(b, slot):
        cp.wait()

    scale = ATT_HEAD_DIM ** -0.5
    q_all = q_ref[0]
    kn_all = kn_ref[0]
    vn_all = vn_ref[0]
    for h in range(ATT_HEADS):
        q = q_all[h:h + 1, :]
        q8 = jnp.broadcast_to(q, (SUBLANES, ATT_HEAD_DIM)).astype(BF16)
        s_own = jnp.sum(q * kn_all[h:h + 1, :], axis=-1, keepdims=True) * scale
        scores = [lax.dot_general(q8, kbuf[slot, h, j].astype(BF16), NT_DIMS, preferred_element_type=F32) * scale
                  for j in range(SEL_PAGES)]
        m = s_own
        for s in scores:
            m = jnp.maximum(m, jnp.max(s, axis=-1, keepdims=True))
        p_own = jnp.exp(s_own - m)
        l = p_own
        acc = p_own * vn_all[h:h + 1, :]
        for j, s in enumerate(scores):
            p = jnp.exp(s - m)
            l = l + jnp.sum(p, axis=-1, keepdims=True)
            acc = acc + jnp.dot(p.astype(BF16), vbuf[slot, h, j].astype(BF16), preferred_element_type=F32)
        o_ref[0, h:h + 1, :] = (acc / l)[0:1, :]


def _moba_sample(q3, k3, v3, cache_k, cache_v, idx_flat, page_table_flat, n_req, n_pages):
    row_spec = pl.BlockSpec((1, ATT_HEADS, ATT_HEAD_DIM), lambda b, idx, pt: (b, 0, 0))
    any_spec = pl.BlockSpec(memory_space=pl.ANY)
    slab_buf = pltpu.VMEM((2, ATT_HEADS, SEL_PAGES, PAGE_SIZE, ATT_HEAD_DIM), F32)
    return pl.pallas_call(
        functools.partial(_moba_sample_kernel, n_req=n_req, n_pages=n_pages),
        grid_spec=pltpu.PrefetchScalarGridSpec(
            num_scalar_prefetch=2,
            grid=(n_req,),
            in_specs=[row_spec, row_spec, row_spec, any_spec, any_spec],
            out_specs=row_spec,
            scratch_shapes=[slab_buf, slab_buf, pltpu.SemaphoreType.DMA((2, 2))],
        ),
        out_shape=jax.ShapeDtypeStruct(q3.shape, F32),
        compiler_params=_params("arbitrary"),
        name="moba_sample",
    )(idx_flat, page_table_flat, q3, k3, v3, cache_k, cache_v)


def _pair_block_diag(w):
    hds, d, _ = w.shape
    wp = w.reshape(hds // 2, 2, d, d)
    z = jnp.zeros((hds // 2, d, d), w.dtype)
    top = jnp.concatenate([wp[:, 0], z], axis=-1)
    bot = jnp.concatenate([z, wp[:, 1]], axis=-1)
    return jnp.concatenate([top, bot], axis=-2)


def _lane_pad(vec, offset):
    out = jnp.zeros((1, LANES), F32)
    return out.at[0, offset:offset + vec.shape[0]].set(vec.astype(F32))


def kernel(x_prompt, x_sample, state_conv, state_rglru_h, state_gdn, cache_k, cache_v, page_table, norm_mix, norm_ffn, norm_final, w_in0, conv0_w, conv0_b, rg_wa, rg_ba, rg_wi, rg_bi, rg_lambda, gdn_a_log, gdn_dt_bias, gdn_norm, w_out0, w_qkv1, w_out1, moe_w_group, moe_b_group, moe_w_router, moe_b_router, moe_w_gate, moe_w_up, moe_w_down):
    bp, tp, d = x_prompt.shape
    bs, ts, _ = x_sample.shape
    n_pages = page_table.shape[1]
    assert ts == 1 and d == D_MODEL
    assert n_pages % PAGES_PER_BLOCK == 0
    assert n_pages // PAGES_PER_BLOCK >= MOBA_TOPK

    w_in0_b = jnp.pad(w_in0, ((0, 0), (0, IN0_PAD - IN0_DIM))).astype(BF16)
    w_out0_b = w_out0.astype(BF16)
    w_qkv1_b = w_qkv1.astype(BF16)
    w_out1_b = w_out1.astype(BF16)
    wg_b, wu_b, wd_b = moe_w_gate.astype(BF16), moe_w_up.astype(BF16), moe_w_down.astype(BF16)
    mix_w = (conv0_w, conv0_b.reshape(1, CONV_CH),
             _pair_block_diag(rg_wa).astype(BF16), rg_ba.reshape(1, RG_WIDTH),
             _pair_block_diag(rg_wi).astype(BF16), rg_bi.reshape(1, RG_WIDTH),
             rg_lambda.reshape(1, RG_WIDTH), _lane_pad(gdn_a_log, GDN_HEADS), _lane_pad(gdn_dt_bias, GDN_HEADS),
             gdn_norm.reshape(1, GDN_DV), w_out0_b)
    router_w = [jnp.pad(jnp.concatenate([moe_w_router[l], moe_w_group[l]], axis=-1),
                        ((0, 0), (0, LANES - MOE_EXPERTS - MOE_GROUPS))) for l in range(2)]
    router_b = [_lane_pad(jnp.concatenate([moe_b_router[l], moe_b_group[l]]), 0) for l in range(2)]

    xp = x_prompt.reshape(bp * tp, d)
    xs8 = jnp.pad(x_sample, ((0, 0), (0, SUBLANES - ts), (0, 0))).reshape(bs * SUBLANES, d)

    (proj_p,) = _norm_matmul(xp, norm_mix[0], w_in0_b, (IN0_PAD,), tm=512)
    (proj_s,) = _norm_matmul(xs8, norm_mix[0], w_in0_b, (IN0_PAD,), tm=bs * SUBLANES)
    xp, p_h, p_gdn = _mixer(proj_p, xp,
                            jnp.zeros((bp, SUBLANES, CONV_CH), F32), jnp.zeros((bp, 1, RG_WIDTH), F32),
                            jnp.zeros((bp, GDN_HEADS, GDN_DK, GDN_DV), F32), mix_w,
                            n_seq=bp, t_len=tp, tc=256, chunk=GDN_CHUNK, t_valid=tp)
    cbuf_s = jnp.pad(state_conv, ((0, 0), (SUBLANES - (CONV_W - 1), 0), (0, 0)))
    xs8, s_h, s_gdn = _mixer(proj_s, xs8, cbuf_s, state_rglru_h.reshape(bs, 1, RG_WIDTH), state_gdn, mix_w,
                             n_seq=bs, t_len=SUBLANES, tc=SUBLANES, chunk=SUBLANES, t_valid=ts)
    xs = xs8.reshape(bs, SUBLANES, d)[:, 0]
    p_conv = proj_p.reshape(bp, tp, IN0_PAD)[:, tp - (CONV_W - 1):, :CONV_CH]
    s_conv = jnp.concatenate([state_conv[:, ts:], proj_s.reshape(bs, SUBLANES, IN0_PAD)[:, :ts, :CONV_CH]], axis=1)

    moe0 = (norm_ffn[0], router_w[0], router_b[0], wg_b[0], wu_b[0], wd_b[0], norm_final)
    pt_flat = page_table.reshape(-1)
    half = (n_pages // (2 * PAGES_PER_BLOCK)) * PAGES_PER_BLOCK
    xp, kmean_lo = _hier_moe(xp, *moe0, tm=MOE_TM, final_norm=False, cache=(cache_k, pt_flat, bs, n_pages, 0, half))
    xs = _hier_moe(xs, *moe0, tm=bs, final_norm=False)

    hd_all = ATT_HEADS * ATT_HEAD_DIM
    moe1 = (norm_ffn[1], router_w[1], router_b[1], wg_b[1], wu_b[1], wd_b[1], norm_final)
    qp, kp, vp = _norm_matmul(xp, norm_mix[1], w_qkv1_b, (hd_all,) * 3, tm=512)
    op = _moba_prompt(qp, kp, vp, bp, tp)
    xp = _matmul_residual(op, w_out1_b, xp, tm=512)
    y_p, kmean_hi = _hier_moe(xp, *moe1, tm=MOE_TM, final_norm=True,
                              cache=(cache_k, pt_flat, bs, n_pages, half, n_pages - half))

    qs, ks, vs = _norm_matmul(xs, norm_mix[1], w_qkv1_b, (hd_all,) * 3, tm=bs)
    row3 = lambda a: a.reshape(bs, ATT_HEADS, ATT_HEAD_DIM)
    idx = _top_blocks(row3(qs), (kmean_lo, kmean_hi))
    os_ = _moba_sample(row3(qs), row3(ks), row3(vs), cache_k, cache_v,
                       idx[:, :, :MOBA_TOPK].reshape(-1), pt_flat, bs, n_pages)
    xs = _matmul_residual(os_.reshape(bs, hd_all), w_out1_b, xs, tm=bs)
    y_s = _hier_moe(xs, *moe1, tm=bs, final_norm=True)

    heads = lambda a, b_, t_: a.reshape(b_, t_, ATT_HEADS, ATT_HEAD_DIM)
    return (y_p.reshape(bp, tp, d), y_s.reshape(bs, ts, d),
            p_conv, p_h.reshape(bp, RG_WIDTH), p_gdn,
            heads(kp, bp, tp), heads(vp, bp, tp),
            s_conv, s_h.reshape(bs, RG_WIDTH), s_gdn,
            heads(ks, bs, ts), heads(vs, bs, ts))
```

```python
import functools

import jax
import jax.numpy as jnp
from jax import lax
from jax.experimental import pallas as pl
from jax.experimental.pallas import tpu as pltpu

F32 = jnp.float32
BF16 = jnp.bfloat16
HIGHEST = lax.Precision.HIGHEST

D_MODEL = 1024
RG_WIDTH = 512
RG_HEADS = 8
RG_HEAD_DIM = 64
RG_C = 8.0
GDN_HEADS = 4
GDN_DK = 128
GDN_DV = 128
GDN_CHUNK = 64
CONV_W = 4
CONV_CH = 2048
IN0_DIM = 3080
IN0_PAD = 3200
ATT_HEADS = 8
ATT_HEAD_DIM = 128
MOBA_BLOCK = 256
MOBA_TOPK = 3
PAGE_SIZE = 128
PAGES_PER_BLOCK = MOBA_BLOCK // PAGE_SIZE
MOE_GROUPS = 4
MOE_EXPERTS_PER_GROUP = 4
MOE_EXPERTS = 16
MOE_FF = 512
RMS_EPS = 1e-6
L2_EPS = 1e-6
NEG_INF = -1e30
LOG2_E = 1.4426950408889634

LANES = 128
SUBLANES = 8
VMEM_LIMIT = 56 * 1024 * 1024
MOE_TM = 1024
NT_DIMS = (((1,), (1,)), ((), ()))
TN_DIMS = (((0,), (0,)), ((), ()))


def _params(*sem):
    return pltpu.CompilerParams(dimension_semantics=sem, vmem_limit_bytes=VMEM_LIMIT)


def _rms(x, g):
    return x * lax.rsqrt(jnp.mean(x * x, axis=-1, keepdims=True) + RMS_EPS) * g


def _softplus(x):
    return jnp.maximum(x, 0.0) + jnp.log1p(jnp.exp(-jnp.abs(x)))


def _sigmoid(x):
    return 1.0 / (1.0 + jnp.exp(-x))


def _norm_matmul_kernel(x_ref, g_ref, w_ref, *out_refs, widths):
    h = _rms(x_ref[...], g_ref[...]).astype(BF16)
    lo = 0
    for o_ref, width in zip(out_refs, widths):
        o_ref[...] = jnp.dot(h, w_ref[:, lo:lo + width], preferred_element_type=F32)
        lo += width


def _norm_matmul(x, g, w_bf16, widths, tm):
    n, d = x.shape
    assert n % tm == 0 and sum(widths) == w_bf16.shape[1]
    return pl.pallas_call(
        functools.partial(_norm_matmul_kernel, widths=widths),
        grid=(n // tm,),
        in_specs=[pl.BlockSpec((tm, d), lambda i: (i, 0)),
                  pl.BlockSpec((1, d), lambda i: (0, 0)),
                  pl.BlockSpec(w_bf16.shape, lambda i: (0, 0))],
        out_specs=[pl.BlockSpec((tm, wd), lambda i: (i, 0)) for wd in widths],
        out_shape=[jax.ShapeDtypeStruct((n, wd), F32) for wd in widths],
        compiler_params=_params("parallel"),
        name="norm_matmul",
    )(x, g.reshape(1, d), w_bf16)


def _matmul_residual_kernel(a_ref, w_ref, r_ref, o_ref):
    o_ref[...] = r_ref[...] + jnp.dot(a_ref[...].astype(BF16), w_ref[...], preferred_element_type=F32)


def _matmul_residual(a, w_bf16, res, tm):
    n, k = a.shape
    m = w_bf16.shape[1]
    assert n % tm == 0
    return pl.pallas_call(
        _matmul_residual_kernel,
        grid=(n // tm,),
        in_specs=[pl.BlockSpec((tm, k), lambda i: (i, 0)),
                  pl.BlockSpec((k, m), lambda i: (0, 0)),
                  pl.BlockSpec((tm, m), lambda i: (i, 0))],
        out_specs=pl.BlockSpec((tm, m), lambda i: (i, 0)),
        out_shape=jax.ShapeDtypeStruct((n, m), F32),
        compiler_params=_params("parallel"),
        name="matmul_residual",
    )(a, w_bf16, res)


def _linear_scan(a, u, n_rows):
    row = lax.broadcasted_iota(jnp.int32, a.shape, 0)
    d = 1
    while d < n_rows:
        a_prev = jnp.where(row >= d, pltpu.roll(a, d, 0), 1.0)
        u_prev = jnp.where(row >= d, pltpu.roll(u, d, 0), 0.0)
        u = a * u_prev + u
        a = a * a_prev
        d *= 2
    return a, u


SPLIT_POWER_MAX = 2


def _bmm(a, b):
    return jnp.einsum('bij,bjk->bik', a.astype(BF16), b.astype(BF16), preferred_element_type=F32)


def _split_bf16(x):
    hi = x.astype(BF16)
    return hi, x - hi.astype(F32)


def _bmm_split(a, b):
    a_hi, a_lo = _split_bf16(a)
    b_hi, b_lo = _split_bf16(b)
    return _bmm(a_hi, b_hi) + (_bmm(a_hi, b_lo) + _bmm(a_lo, b_hi))


def _unit_lower_solve(lmat, rhs, c):
    m = -lmat
    sol = rhs + _bmm_split(m, rhs)
    p = 2
    while p < c:
        mm = _bmm_split if p <= SPLIT_POWER_MAX else _bmm
        m = mm(m, m)
        sol = sol + mm(m, sol)
        p *= 2
    return sol


def _mixer_kernel(conv_ref, gate_ref, z_ref, ba_ref, x_ref, cbuf0_ref, h0_ref, s0_ref,
                  cw_ref, cb_ref, wa_ref, rba_ref, wi_ref, rbi_ref, lam_ref, alog_ref, dtb_ref, gnorm_ref, wout_ref,
                  x1_ref, ht_ref, st_ref,
                  cbuf, hcar, state, act, *, tc, chunk, t_valid, nt):
    t = pl.program_id(1)

    @pl.when(t == 0)
    def _():
        cbuf[0:SUBLANES, :] = cbuf0_ref[0]
        hcar[...] = h0_ref[0]
        state[...] = s0_ref[0]

    cbuf[SUBLANES:SUBLANES + tc, :] = conv_ref[...]
    base = SUBLANES - (CONV_W - 1)
    y = cb_ref[...] + cbuf[base:base + tc, :] * cw_ref[0:1, :]
    for j in range(1, CONV_W):
        y = y + cbuf[base + j:base + j + tc, :] * cw_ref[j:j + 1, :]
    cbuf[0:SUBLANES, :] = cbuf[tc:tc + SUBLANES, :]

    masked = t_valid < nt * tc
    if masked:
        valid = (t * tc + lax.broadcasted_iota(jnp.int32, (tc, 1), 0)) < t_valid

    xr = y[:, :RG_WIDTH]
    xr_b = xr.astype(BF16)
    pair = 2 * RG_HEAD_DIM
    r_lin = jnp.concatenate(
        [jnp.dot(xr_b[:, pair * i:pair * (i + 1)], wa_ref[i], preferred_element_type=F32) for i in range(RG_WIDTH // pair)], axis=-1)
    i_lin = jnp.concatenate(
        [jnp.dot(xr_b[:, pair * i:pair * (i + 1)], wi_ref[i], preferred_element_type=F32) for i in range(RG_WIDTH // pair)], axis=-1)
    r = _sigmoid(r_lin + rba_ref[...])
    gi = _sigmoid(i_lin + rbi_ref[...])
    log_a = -RG_C * r * _softplus(-lam_ref[...])
    a = jnp.exp(log_a)
    u = jnp.sqrt(-jnp.tanh(log_a) * (a * a + 1.0)) * (gi * xr)
    if masked:
        a = jnp.where(valid, a, 1.0)
        u = jnp.where(valid, u, 0.0)
    a_cum, h_loc = _linear_scan(a, u, tc)
    h = h_loc + a_cum * hcar[...]
    hcar[...] = h[tc - 1:tc, :]
    act[:, 0:RG_WIDTH] = (h * jax.nn.gelu(gate_ref[...])).astype(BF16)

    qkv = y[:, RG_WIDTH:]
    qkv = qkv * _sigmoid(qkv)
    ba = ba_ref[...]
    beta_all = _sigmoid(ba)
    g_all = -jnp.exp(alog_ref[...]) * _softplus(ba + dtb_ref[...])
    if masked:
        beta_all = jnp.where(valid, beta_all, 0.0)
        g_all = jnp.where(valid, g_all, 0.0)
    in_chunk = lax.broadcasted_iota(jnp.int32, g_all.shape, 0) & (chunk - 1)
    gcum_all = g_all
    d = 1
    while d < chunk:
        gcum_all = gcum_all + jnp.where(in_chunk >= d, pltpu.roll(gcum_all, d, 0), 0.0)
        d *= 2

    nc = tc // chunk

    def chunk_head_batch(cols_of_head):
        per_head = [cols_of_head(hd) for hd in range(GDN_HEADS)]
        return jnp.concatenate([per_head[hd][ci * chunk:(ci + 1) * chunk][None]
                                for ci in range(nc) for hd in range(GDN_HEADS)], axis=0)

    def l2_normalised(x):
        return x * lax.rsqrt(jnp.sum(x * x, axis=-1, keepdims=True) + L2_EPS)

    q_b = chunk_head_batch(lambda hd: l2_normalised(qkv[:, GDN_DK * hd:GDN_DK * (hd + 1)]) * (GDN_DK ** -0.5))
    k_b = chunk_head_batch(lambda hd: l2_normalised(qkv[:, 512 + GDN_DK * hd:512 + GDN_DK * (hd + 1)]))
    v_b = chunk_head_batch(lambda hd: qkv[:, 1024 + GDN_DV * hd:1024 + GDN_DV * (hd + 1)])
    beta_b = chunk_head_batch(lambda hd: beta_all[:, hd:hd + 1])
    gcol = chunk_head_batch(lambda hd: gcum_all[:, GDN_HEADS + hd:GDN_HEADS + hd + 1])

    nbatch = nc * GDN_HEADS
    crow = lax.broadcasted_iota(jnp.int32, (nbatch, chunk, chunk), 1)
    ccol = lax.broadcasted_iota(jnp.int32, (nbatch, chunk, chunk), 2)
    lower = crow >= ccol
    strict = crow > ccol
    grow = jnp.sum(jnp.where(crow == ccol, gcol, 0.0), axis=1, keepdims=True)
    decay = jnp.where(lower, jnp.exp(jnp.where(lower, gcol - grow, 0.0)), 0.0)
    kb = k_b * beta_b
    k_bf = k_b.astype(BF16)
    kk = jnp.einsum('bik,bjk->bij', kb.astype(BF16), k_bf, preferred_element_type=F32)
    rhs = jnp.concatenate([v_b * beta_b, kb * jnp.exp(gcol)], axis=-1)
    sol = _unit_lower_solve(jnp.where(strict, kk * decay, 0.0), rhs, chunk)
    qk = jnp.where(lower, jnp.einsum('bik,bjk->bij', q_b.astype(BF16), k_bf, preferred_element_type=F32) * decay, 0.0)
    qk_bf = qk.astype(BF16)
    q_dec = (q_b * jnp.exp(gcol)).astype(BF16)
    g_last = gcol[:, chunk - 1:chunk, :]
    k_dec = (k_b * jnp.exp(g_last - gcol)).astype(BF16)
    s_decay = jnp.exp(g_last)
    u_all = sol[:, :, :GDN_DV]
    w_bf = sol[:, :, GDN_DV:].astype(BF16)

    gnorm = gnorm_ref[...]
    z_all = z_ref[...]
    s_heads = [state[hd] for hd in range(GDN_HEADS)]
    for ci in range(nc):
        lo = ci * chunk
        for hd in range(GDN_HEADS):
            bi = ci * GDN_HEADS + hd
            s_h = s_heads[hd]
            s_bf = s_h.astype(BF16)
            v_new = u_all[bi] - jnp.dot(w_bf[bi], s_bf, preferred_element_type=F32)
            v_new_bf = v_new.astype(BF16)
            o = (jnp.dot(q_dec[bi], s_bf, preferred_element_type=F32)
                 + jnp.dot(qk_bf[bi], v_new_bf, preferred_element_type=F32))
            s_heads[hd] = s_h * s_decay[bi] + lax.dot_general(k_dec[bi], v_new_bf, TN_DIMS, preferred_element_type=F32)
            zh = z_all[lo:lo + chunk, GDN_DV * hd:GDN_DV * (hd + 1)]
            o = _rms(o, gnorm) * (zh * _sigmoid(zh))
            act[lo:lo + chunk, RG_WIDTH + GDN_DV * hd:RG_WIDTH + GDN_DV * (hd + 1)] = o.astype(BF16)
    for hd in range(GDN_HEADS):
        state[hd] = s_heads[hd]

    x1_ref[...] = x_ref[...] + jnp.dot(act[...], wout_ref[...], preferred_element_type=F32)

    @pl.when(t == nt - 1)
    def _():
        ht_ref[0] = hcar[...]
        st_ref[0] = state[...]


def _mixer(proj, x, cbuf0, h0, s0, wts, *, n_seq, t_len, tc, chunk, t_valid):
    nt = t_len // tc
    assert t_len % tc == 0 and tc % chunk == 0 and tc % SUBLANES == 0
    row = lambda b, t: b * nt + t
    const2 = lambda b, t: (0, 0)
    (cw, cb, wa4, rba, wi4, rbi, lam, alog, dtb, gnorm, wout) = wts
    in_specs = [
        pl.BlockSpec((tc, CONV_CH), lambda b, t: (row(b, t), 0)),
        pl.BlockSpec((tc, RG_WIDTH), lambda b, t: (row(b, t), CONV_CH // RG_WIDTH)),
        pl.BlockSpec((tc, RG_WIDTH), lambda b, t: (row(b, t), CONV_CH // RG_WIDTH + 1)),
        pl.BlockSpec((tc, LANES), lambda b, t: (row(b, t), (CONV_CH + 2 * RG_WIDTH) // LANES)),
        pl.BlockSpec((tc, D_MODEL), lambda b, t: (row(b, t), 0)),
        pl.BlockSpec((1, SUBLANES, CONV_CH), lambda b, t: (b, 0, 0)),
        pl.BlockSpec((1, 1, RG_WIDTH), lambda b, t: (b, 0, 0)),
        pl.BlockSpec((1, GDN_HEADS, GDN_DK, GDN_DV), lambda b, t: (b, 0, 0, 0)),
        pl.BlockSpec(cw.shape, const2), pl.BlockSpec(cb.shape, const2),
        pl.BlockSpec(wa4.shape, lambda b, t: (0, 0, 0)), pl.BlockSpec(rba.shape, const2),
        pl.BlockSpec(wi4.shape, lambda b, t: (0, 0, 0)), pl.BlockSpec(rbi.shape, const2),
        pl.BlockSpec(lam.shape, const2), pl.BlockSpec(alog.shape, const2), pl.BlockSpec(dtb.shape, const2),
        pl.BlockSpec(gnorm.shape, const2), pl.BlockSpec(wout.shape, const2),
    ]
    out_specs = [
        pl.BlockSpec((tc, D_MODEL), lambda b, t: (row(b, t), 0)),
        pl.BlockSpec((1, 1, RG_WIDTH), lambda b, t: (b, 0, 0)),
        pl.BlockSpec((1, GDN_HEADS, GDN_DK, GDN_DV), lambda b, t: (b, 0, 0, 0)),
    ]
    out_shape = [
        jax.ShapeDtypeStruct((n_seq * t_len, D_MODEL), F32),
        jax.ShapeDtypeStruct((n_seq, 1, RG_WIDTH), F32),
        jax.ShapeDtypeStruct((n_seq, GDN_HEADS, GDN_DK, GDN_DV), F32),
    ]
    return pl.pallas_call(
        functools.partial(_mixer_kernel, tc=tc, chunk=chunk, t_valid=t_valid, nt=nt),
        grid=(n_seq, nt),
        in_specs=in_specs, out_specs=out_specs, out_shape=out_shape,
        scratch_shapes=[pltpu.VMEM((tc + SUBLANES, CONV_CH), F32),
                        pltpu.VMEM((1, RG_WIDTH), F32),
                        pltpu.VMEM((GDN_HEADS, GDN_DK, GDN_DV), F32),
                        pltpu.VMEM((tc, D_MODEL), BF16)],
        compiler_params=_params("arbitrary", "arbitrary"),
        name="mixer0",
    )(proj, proj, proj, proj, x, cbuf0, h0, s0, cw, cb, wa4, rba, wi4, rbi, lam, alog, dtb, gnorm, wout)


def _route(h, w_ref, b_ref):
    logits = jnp.dot(h, w_ref[...], precision=HIGHEST, preferred_element_type=F32) + b_ref[...]
    lane = lax.broadcasted_iota(jnp.int32, logits.shape, 1)
    big = jnp.int32(1 << 20)
    is_grp = (lane >= MOE_EXPERTS) & (lane < MOE_EXPERTS + MOE_GROUPS)
    gl = jnp.where(is_grp, logits, -jnp.inf)
    gmax = jnp.max(gl, axis=-1, keepdims=True)
    gidx = jnp.min(jnp.where(gl == gmax, lane, big), axis=-1, keepdims=True) - MOE_EXPERTS
    g_top = 1.0 / jnp.sum(jnp.exp(gl - gmax), axis=-1, keepdims=True)
    in_grp = (lane >= gidx * MOE_EXPERTS_PER_GROUP) & (lane < (gidx + 1) * MOE_EXPERTS_PER_GROUP)
    el = jnp.where(in_grp, logits, -jnp.inf)
    m1 = jnp.max(el, axis=-1, keepdims=True)
    i1 = jnp.min(jnp.where(el == m1, lane, big), axis=-1, keepdims=True)
    el2 = jnp.where(lane == i1, -jnp.inf, el)
    m2 = jnp.max(el2, axis=-1, keepdims=True)
    i2 = jnp.min(jnp.where(el2 == m2, lane, big), axis=-1, keepdims=True)
    e2 = jnp.exp(m2 - m1)
    w1 = g_top / (1.0 + e2)
    w2 = g_top * e2 / (1.0 + e2)
    return jnp.where(lane == i1, w1, 0.0) + jnp.where(lane == i2, w2, 0.0), gidx


def _router_kernel(x_ref, g_ref, w_ref, b_ref, h_ref, gates_ref):
    h = _rms(x_ref[...], g_ref[...])
    h_ref[...] = h.astype(BF16)
    gates_ref[...], _ = _route(h, w_ref, b_ref)


def _moe_router(x, g, w_cat, b_cat, tm):
    n, d = x.shape
    return pl.pallas_call(
        _router_kernel,
        grid=(n // tm,),
        in_specs=[pl.BlockSpec((tm, d), lambda i: (i, 0)),
                  pl.BlockSpec((1, d), lambda i: (0, 0)),
                  pl.BlockSpec((d, LANES), lambda i: (0, 0)),
                  pl.BlockSpec((1, LANES), lambda i: (0, 0))],
        out_specs=[pl.BlockSpec((tm, d), lambda i: (i, 0)),
                   pl.BlockSpec((tm, LANES), lambda i: (i, 0))],
        out_shape=[jax.ShapeDtypeStruct((n, d), BF16), jax.ShapeDtypeStruct((n, LANES), F32)],
        compiler_params=_params("parallel"),
        name="moe_router",
    )(x, g.reshape(1, d), w_cat, b_cat)


def _block_means(page_refs, o_ref):
    for b in range(len(page_refs) // PAGES_PER_BLOCK):
        s = jnp.sum(page_refs[PAGES_PER_BLOCK * b][0], axis=0)
        for r in range(1, PAGES_PER_BLOCK):
            s = s + jnp.sum(page_refs[PAGES_PER_BLOCK * b + r][0], axis=0)
        o_ref[0, b] = s / MOBA_BLOCK


def _moe_expert_kernel(*refs, final_norm, n_cache_pages):
    if n_cache_pages:
        refs = refs[1:]
    h_ref, gates_ref, x_ref, wg_ref, wu_ref, wd_ref, gf_ref = refs[:7]
    page_refs = refs[7:7 + n_cache_pages]
    o_ref = refs[7 + n_cache_pages]
    e = pl.program_id(1)

    if n_cache_pages:
        _block_means(page_refs, refs[8 + n_cache_pages])

    @pl.when(e == 0)
    def _():
        o_ref[...] = x_ref[...]

    h = h_ref[...]
    a = jnp.dot(h, wg_ref[0], preferred_element_type=F32)
    b = jnp.dot(h, wu_ref[0], preferred_element_type=F32)
    gates = gates_ref[...]
    lane = lax.broadcasted_iota(jnp.int32, gates.shape, 1)
    gcol = jnp.sum(jnp.where(lane == e, gates, 0.0), axis=-1, keepdims=True)
    hid = (a * _sigmoid(a)) * b * gcol
    o_ref[...] += jnp.dot(hid.astype(BF16), wd_ref[0], preferred_element_type=F32)

    if final_norm:
        @pl.when(e == MOE_EXPERTS - 1)
        def _():
            o_ref[...] = _rms(o_ref[...], gf_ref[...])


def _moe_experts(h, gates, x, wg, wu, wd, g_final, tm, final_norm, cache=None):
    n, d = x.shape
    grid = (n // tm, MOE_EXPERTS)
    in_specs = [pl.BlockSpec((tm, d), lambda i, e, *_: (i, 0)),
                pl.BlockSpec((tm, LANES), lambda i, e, *_: (i, 0)),
                pl.BlockSpec((tm, d), lambda i, e, *_: (i, 0)),
                pl.BlockSpec((1, d, MOE_FF), lambda i, e, *_: (e, 0, 0)),
                pl.BlockSpec((1, d, MOE_FF), lambda i, e, *_: (e, 0, 0)),
                pl.BlockSpec((1, MOE_FF, d), lambda i, e, *_: (e, 0, 0)),
                pl.BlockSpec((1, d), lambda i, e, *_: (0, 0))]
    out_specs = [pl.BlockSpec((tm, d), lambda i, e, *_: (i, 0))]
    out_shape = [jax.ShapeDtypeStruct((n, d), F32)]
    args = [h, gates, x, wg, wu, wd, g_final.reshape(1, d)]
    prefetch = []
    n_cache_pages = 0
    if cache is not None:
        cache_k, pt_flat, n_req, n_pages, first_page, span = cache
        steps = grid[0] * grid[1]
        assert (n_req * span) % steps == 0 and steps % n_req == 0
        n_cache_pages = n_req * span // steps
        groups = steps // n_req
        assert n_cache_pages % PAGES_PER_BLOCK == 0 and first_page % PAGES_PER_BLOCK == 0
        prefetch = [pt_flat]

        def page_index(i, e, pt, r):
            step = i * MOE_EXPERTS + e
            return (pt[(step // groups) * n_pages + first_page + (step % groups) * n_cache_pages + r], 0, 0, 0)

        for r in range(n_cache_pages):
            in_specs.append(pl.BlockSpec((1, PAGE_SIZE, ATT_HEADS, ATT_HEAD_DIM), functools.partial(page_index, r=r)))
        args += [cache_k] * n_cache_pages
        blocks_per_step = n_cache_pages // PAGES_PER_BLOCK
        out_specs.append(pl.BlockSpec(
            (1, blocks_per_step, ATT_HEADS, ATT_HEAD_DIM),
            lambda i, e, *_: ((i * MOE_EXPERTS + e) // groups, (i * MOE_EXPERTS + e) % groups, 0, 0)))
        out_shape.append(jax.ShapeDtypeStruct((n_req, span // PAGES_PER_BLOCK, ATT_HEADS, ATT_HEAD_DIM), F32))
    outs = pl.pallas_call(
        functools.partial(_moe_expert_kernel, final_norm=final_norm, n_cache_pages=n_cache_pages),
        grid_spec=pltpu.PrefetchScalarGridSpec(
            num_scalar_prefetch=len(prefetch), grid=grid, in_specs=in_specs, out_specs=out_specs),
        out_shape=out_shape,
        compiler_params=_params("parallel", "arbitrary"),
        name="moe_experts",
    )(*prefetch, *args)
    return outs if cache is not None else outs[0]


def _hier_moe(x, g_norm, w_cat, b_cat, wg, wu, wd, g_final, tm, final_norm, cache=None):
    h, gates = _moe_router(x, g_norm, w_cat, b_cat, tm)
    return _moe_experts(h, gates, x, wg, wu, wd, g_final, tm, final_norm, cache)


SORT_TILE = 256
GRANULE = 16
SORT_ROWS = SORT_TILE + MOE_GROUPS * GRANULE
GROUP_TILE = 512
GRANULES_PER_TILE = GROUP_TILE // GRANULE
CACHE_PAGES_PER_STEP = 16


def _dispatch_kernel(x_ref, g_ref, w_ref, b_ref, hs_ref, gs_ref, slot_ref, cnt_ref):
    h = _rms(x_ref[...], g_ref[...])
    gates, gidx = _route(h, w_ref, b_ref)
    r = h.shape[0]
    lane = lax.broadcasted_iota(jnp.int32, (r, LANES), 1)
    onehot = jnp.where(lane == gidx, 1.0, 0.0)
    tok = lax.broadcasted_iota(jnp.int32, (r, r), 0)
    other = lax.broadcasted_iota(jnp.int32, (r, r), 1)
    earlier = jnp.where(other < tok, 1.0, 0.0).astype(BF16)
    prefix = jnp.dot(earlier, onehot.astype(BF16), preferred_element_type=F32)
    rank = jnp.sum(prefix * onehot, axis=-1, keepdims=True)
    counts = jnp.sum(onehot, axis=0, keepdims=True)
    padded = jnp.floor((counts + (GRANULE - 1.0)) * (1.0 / GRANULE)) * GRANULE
    lane1 = lax.broadcasted_iota(jnp.int32, (1, LANES), 1)
    seg_start = jnp.zeros((1, LANES), F32)
    running = jnp.zeros((1, 1), F32)
    for grp in range(MOE_GROUPS):
        seg_start = jnp.where(lane1 == grp, running, seg_start)
        running = running + padded[:, grp:grp + 1]
    slot = jnp.sum(onehot * seg_start, axis=-1, keepdims=True) + rank
    slot_b = jnp.broadcast_to(slot, (r, LANES))
    slot_row = slot_b.T[0:1, :]
    srow = lax.broadcasted_iota(jnp.int32, (SORT_ROWS, r), 0).astype(F32)
    perm = jnp.where(srow == slot_row, 1.0, 0.0)
    hs_ref[...] = jnp.dot(perm.astype(BF16), h.astype(BF16), preferred_element_type=F32).astype(BF16)
    gs_ref[...] = jnp.dot(perm, gates, precision=HIGHEST, preferred_element_type=F32)
    slot_ref[...] = slot_b.astype(jnp.int32)
    cnt_ref[0] = jnp.broadcast_to(counts, (SUBLANES, LANES)).astype(jnp.int32)


def _moe_dispatch(x, g, w_cat, b_cat):
    n, d = x.shape
    nt = n // SORT_TILE
    return pl.pallas_call(
        _dispatch_kernel,
        grid=(nt,),
        in_specs=[pl.BlockSpec((SORT_TILE, d), lambda i: (i, 0)),
                  pl.BlockSpec((1, d), lambda i: (0, 0)),
                  pl.BlockSpec((d, LANES), lambda i: (0, 0)),
                  pl.BlockSpec((1, LANES), lambda i: (0, 0))],
        out_specs=[pl.BlockSpec((SORT_ROWS, d), lambda i: (i, 0)),
                   pl.BlockSpec((SORT_ROWS, LANES), lambda i: (i, 0)),
                   pl.BlockSpec((SORT_TILE, LANES), lambda i: (i, 0)),
                   pl.BlockSpec((1, SUBLANES, LANES), lambda i: (i, 0, 0))],
        out_shape=[jax.ShapeDtypeStruct((nt * SORT_ROWS, d), BF16),
                   jax.ShapeDtypeStruct((nt * SORT_ROWS, LANES), F32),
                   jax.ShapeDtypeStruct((n, LANES), jnp.int32),
                   jax.ShapeDtypeStruct((nt, SUBLANES, LANES), jnp.int32)],
        compiler_params=_params("parallel"),
        name="moe_dispatch",
    )(x, g.reshape(1, d), w_cat, b_cat)


def _group_tile_tables(cnt, n_group_tiles):
    nt, ng = cnt.shape
    pc = (cnt + GRANULE - 1) // GRANULE * GRANULE
    seg_row = jnp.cumsum(pc, axis=1) - pc
    ngr = (pc // GRANULE).T.reshape(-1)
    seg_end = jnp.cumsum(ngr)
    seg_first = seg_end - ngr
    k = jnp.arange(nt * SORT_ROWS // GRANULE)
    seg = jnp.minimum(jnp.searchsorted(seg_end, k, side='right'), ng * nt - 1)
    gi, ti = seg // nt, seg % nt
    gran_row = ti * SORT_ROWS + seg_row[ti, gi] + (k - seg_first[seg]) * GRANULE
    gran_row = jnp.where(k < seg_end[-1], gran_row, 0)
    group_gran = jnp.sum(ngr.reshape(ng, nt), axis=1)
    group_first = jnp.cumsum(group_gran) - group_gran
    tiles = (group_gran + GRANULES_PER_TILE - 1) // GRANULES_PER_TILE
    tile_end = jnp.cumsum(tiles)
    j = jnp.arange(n_group_tiles)
    tg = jnp.minimum(jnp.searchsorted(tile_end, j, side='right'), ng - 1)
    local = j - (tile_end[tg] - tiles[tg])
    first = group_first[tg] + local * GRANULES_PER_TILE
    count = jnp.clip(group_gran[tg] - local * GRANULES_PER_TILE, 0, GRANULES_PER_TILE)
    count = jnp.where(j < tile_end[-1], count, 0)
    i32 = lambda a: a.astype(jnp.int32)
    return i32(gran_row), i32(tg), i32(first), i32(count)


def _moe_group_kernel(*refs, n_cache_pages):
    gran_ref, tgrp_ref, tfirst_ref, tcount_ref = refs[:4]
    refs = refs[5:] if n_cache_pages else refs[4:]
    wg_ref, wu_ref, wd_ref, hs_ref, gs_ref, _ = refs[:6]
    page_refs = refs[6:6 + n_cache_pages]
    y_ref = refs[6 + n_cache_pages]
    refs = refs[7 + n_cache_pages:]
    if n_cache_pages:
        _block_means(page_refs, refs[0])
        refs = refs[1:]
    hbuf, gbuf, ybuf, sem = refs
    j = pl.program_id(0)
    e = pl.program_id(1)
    count = tcount_ref[j]
    first = tfirst_ref[j]

    def buf_rows(i):
        return pl.ds(pl.multiple_of(i * GRANULE, GRANULE), GRANULE)

    def sorted_rows(i):
        return pl.ds(pl.multiple_of(gran_ref[first + i], GRANULE), GRANULE)

    def gather_copies(i):
        return (pltpu.make_async_copy(hs_ref.at[sorted_rows(i), :], hbuf.at[buf_rows(i), :], sem.at[0]),
                pltpu.make_async_copy(gs_ref.at[sorted_rows(i), :], gbuf.at[buf_rows(i), :], sem.at[1]))

    def scatter_copy(i):
        return pltpu.make_async_copy(ybuf.at[buf_rows(i), :], y_ref.at[sorted_rows(i), :], sem.at[2])

    def for_granules(lo, hi, fn):
        def body(i, carry):
            fn(i)
            return carry
        lax.fori_loop(lo, hi, body, 0)

    @pl.when(jnp.logical_and(e == 0, count > 0))
    def _():
        def start(i):
            for cp in gather_copies(i):
                cp.start()

        def wait(i):
            for cp in gather_copies(i):
                cp.wait()

        def clear(i):
            hbuf[buf_rows(i), :] = jnp.zeros((GRANULE, hbuf.shape[1]), hbuf.dtype)
            gbuf[buf_rows(i), :] = jnp.zeros((GRANULE, gbuf.shape[1]), gbuf.dtype)

        for_granules(0, count, start)
        for_granules(0, count, wait)
        for_granules(count, GRANULES_PER_TILE, clear)
        ybuf[...] = jnp.zeros_like(ybuf)

    @pl.when(count > 0)
    def _():
        h = hbuf[...]
        a = jnp.dot(h, wg_ref[0], preferred_element_type=F32)
        b = jnp.dot(h, wu_ref[0], preferred_element_type=F32)
        gates = gbuf[...]
        lane = lax.broadcasted_iota(jnp.int32, gates.shape, 1)
        expert = tgrp_ref[j] * MOE_EXPERTS_PER_GROUP + e
        gcol = jnp.sum(jnp.where(lane == expert, gates, 0.0), axis=-1, keepdims=True)
        hid = (a * _sigmoid(a)) * b * gcol
        ybuf[...] += jnp.dot(hid.astype(BF16), wd_ref[0], preferred_element_type=F32)

    @pl.when(jnp.logical_and(e == MOE_EXPERTS_PER_GROUP - 1, count > 0))
    def _():
        for_granules(0, count, lambda i: scatter_copy(i).start())
        for_granules(0, count, lambda i: scatter_copy(i).wait())


def _moe_group_experts(tables, hs, gs, wg, wu, wd, n_group_tiles, cache):
    rows, d = hs.shape
    grid = (n_group_tiles, MOE_EXPERTS_PER_GROUP)

    def weight_index(j, e, gran, tgrp, tfirst, tcount, *_):
        return (tgrp[j] * MOE_EXPERTS_PER_GROUP + jnp.where(tcount[j] > 0, e, MOE_EXPERTS_PER_GROUP - 1), 0, 0)

    any_spec = pl.BlockSpec(memory_space=pl.ANY)
    in_specs = [pl.BlockSpec((1, d, MOE_FF), weight_index),
                pl.BlockSpec((1, d, MOE_FF), weight_index),
                pl.BlockSpec((1, MOE_FF, d), weight_index),
                any_spec, any_spec, any_spec]
    prefetch = list(tables)
    args = [wg, wu, wd, hs, gs, jnp.zeros((rows, d), F32)]
    out_specs = [any_spec]
    out_shape = [jax.ShapeDtypeStruct((rows, d), F32)]
    n_cache_pages = 0
    if cache is not None:
        cache_k, pt_flat, n_req, n_pages, first_page, span = cache
        n_cache_pages = CACHE_PAGES_PER_STEP
        assert span % n_cache_pages == 0 and n_cache_pages % PAGES_PER_BLOCK == 0 and first_page % PAGES_PER_BLOCK == 0
        groups = span // n_cache_pages
        stream_steps = n_req * groups
        assert stream_steps <= grid[0] * grid[1]
        prefetch.append(pt_flat)

        def stream_step(j, e):
            return jnp.minimum(j * MOE_EXPERTS_PER_GROUP + e, stream_steps - 1)

        def page_index(j, e, gran, tgrp, tfirst, tcount, pt, r):
            s = stream_step(j, e)
            return (pt[(s // groups) * n_pages + first_page + (s % groups) * n_cache_pages + r], 0, 0, 0)

        for r in range(n_cache_pages):
            in_specs.append(pl.BlockSpec((1, PAGE_SIZE, ATT_HEADS, ATT_HEAD_DIM), functools.partial(page_index, r=r)))
        args += [cache_k] * n_cache_pages
        blocks_per_step = n_cache_pages // PAGES_PER_BLOCK
        out_specs.append(pl.BlockSpec((1, blocks_per_step, ATT_HEADS, ATT_HEAD_DIM),
                                      lambda j, e, *_: (stream_step(j, e) // groups, stream_step(j, e) % groups, 0, 0)))
        out_shape.append(jax.ShapeDtypeStruct((n_req, span // PAGES_PER_BLOCK, ATT_HEADS, ATT_HEAD_DIM), F32))
    zero_arg = len(prefetch) + 5
    outs = pl.pallas_call(
        functools.partial(_moe_group_kernel, n_cache_pages=n_cache_pages),
        grid_spec=pltpu.PrefetchScalarGridSpec(
            num_scalar_prefetch=len(prefetch), grid=grid, in_specs=in_specs, out_specs=out_specs,
            scratch_shapes=[pltpu.VMEM((GROUP_TILE, d), BF16), pltpu.VMEM((GROUP_TILE, LANES), F32),
                            pltpu.VMEM((GROUP_TILE, d), F32), pltpu.SemaphoreType.DMA((3,))]),
        out_shape=out_shape,
        input_output_aliases={zero_arg: 0},
        compiler_params=_params("arbitrary", "arbitrary"),
        name="moe_group_experts",
    )(*prefetch, *args)
    return outs if cache is not None else outs[0]


def _combine_kernel(x_ref, y_ref, slot_ref, gf_ref, o_ref, *, final_norm):
    slot = slot_ref[:, 0:1]
    col = lax.broadcasted_iota(jnp.int32, (x_ref.shape[0], SORT_ROWS), 1)
    unperm = jnp.where(col == slot, 1.0, 0.0)
    out = x_ref[...] + jnp.dot(unperm, y_ref[...], precision=HIGHEST, preferred_element_type=F32)
    if final_norm:
        out = _rms(out, gf_ref[...])
    o_ref[...] = out


def _moe_combine(x, y_sorted, slot, g_final, final_norm):
    n, d = x.shape
    return pl.pallas_call(
        functools.partial(_combine_kernel, final_norm=final_norm),
        grid=(n // SORT_TILE,),
        in_specs=[pl.BlockSpec((SORT_TILE, d), lambda i: (i, 0)),
                  pl.BlockSpec((SORT_ROWS, d), lambda i: (i, 0)),
                  pl.BlockSpec((SORT_TILE, LANES), lambda i: (i, 0)),
                  pl.BlockSpec((1, d), lambda i: (0, 0))],
        out_specs=pl.BlockSpec((SORT_TILE, d), lambda i: (i, 0)),
        out_shape=jax.ShapeDtypeStruct((n, d), F32),
        compiler_params=_params("parallel"),
        name="moe_combine",
    )(x, y_sorted, slot, g_final.reshape(1, d))


def _sorted_moe(x, g_norm, w_cat, b_cat, wg, wu, wd, g_final, final_norm, cache=None):
    n = x.shape[0]
    assert n % SORT_TILE == 0
    hs, gs, slot, cnt = _moe_dispatch(x, g_norm, w_cat, b_cat)
    n_granules = hs.shape[0] // GRANULE
    n_group_tiles = -(-n_granules // GRANULES_PER_TILE) + MOE_GROUPS
    tables = _group_tile_tables(cnt[:, 0, :MOE_GROUPS], n_group_tiles)
    outs = _moe_group_experts(tables, hs, gs, wg, wu, wd, n_group_tiles, cache)
    y_sorted = outs[0] if cache is not None else outs
    out = _moe_combine(x, y_sorted, slot, g_final, final_norm)
    return (out, outs[1]) if cache is not None else out


def _moba_prompt_kernel(q_ref, k_ref, v_ref, o_ref, *, t_len):
    nb = t_len // MOBA_BLOCK
    blk = MOBA_BLOCK
    c = (ATT_HEAD_DIM ** -0.5) * LOG2_E
    kf = k_ref[...]
    k_bf = kf.astype(BF16)
    q_t = q_ref[...].T
    q_t_bf = (q_t * c).astype(BF16)
    v_t_bf = v_ref[...].T.astype(BF16)
    kmean = jnp.mean(kf.reshape(nb, blk, ATT_HEAD_DIM), axis=1)
    key_pos =lax.broadcasted_iota(jnp.int32, (blk, blk), 0)
    query_pos = lax.broadcasted_iota(jnp.int32, (blk, blk), 1)
    causal = key_pos <= query_pos
    for i in range(nb):
        qs = slice(i * blk, (i + 1) * blk)
        sel = [None] * i
        if i > MOBA_TOPK:
            gate = jnp.dot(kmean, q_t[:, qs], precision=HIGHEST, preferred_element_type=F32)
            g = [gate[j:j + 1, :] for j in range(i)]
            rank = [jnp.zeros((1, blk), F32) for _ in range(i)]
            for lo_j in range(i):
                for hi_j in range(lo_j + 1, i):
                    lo_wins = jnp.where(g[lo_j] >= g[hi_j], 1.0, 0.0)
                    rank[hi_j] = rank[hi_j] + lo_wins
                    rank[lo_j] = rank[lo_j] + (1.0 - lo_wins)
            sel = [r < float(MOBA_TOPK) for r in rank]
        pieces = []
        s_max = None
        for j in range(i + 1):
            s = jnp.dot(k_bf[j * blk:(j + 1) * blk], q_t_bf[:, qs], preferred_element_type=F32)
            if j == i:
                s = jnp.where(causal, s, NEG_INF)
            elif sel[j] is not None:
                s = jnp.where(sel[j], s, NEG_INF)
            pieces.append(s)
            s_max = s if s_max is None else jnp.maximum(s_max, s)
        m = jnp.max(s_max, axis=0, keepdims=True)
        p_sum = jnp.zeros((blk, blk), F32)
        acc = jnp.zeros((ATT_HEAD_DIM, blk), F32)
        for j, s in enumerate(pieces):
            p = jnp.exp2(s - m)
            p_sum = p_sum + p
            acc = acc + jnp.dot(v_t_bf[:, j * blk:(j + 1) * blk], p.astype(BF16), preferred_element_type=F32)
        o_ref[qs, :] = (acc / jnp.sum(p_sum, axis=0, keepdims=True)).T


def _moba_prompt(q, k, v, n_seq, t_len):
    assert t_len % MOBA_BLOCK == 0
    spec = pl.BlockSpec((t_len, ATT_HEAD_DIM), lambda b, h: (b, h))
    return pl.pallas_call(
        functools.partial(_moba_prompt_kernel, t_len=t_len),
        grid=(n_seq, ATT_HEADS),
        in_specs=[spec, spec, spec],
        out_specs=spec,
        out_shape=jax.ShapeDtypeStruct(q.shape, F32),
        compiler_params=_params("parallel", "parallel"),
        name="moba_prompt",
    )(q, k, v)


def _top_blocks_kernel(q_ref, *refs):
    kmean_refs, idx_ref = refs[:-1], refs[-1]
    q = q_ref[0]
    kmean = jnp.concatenate([r[0] for r in kmean_refs], axis=0)
    gate = jnp.sum(kmean * q[None, :, :], axis=-1, keepdims=True)
    n_blocks = gate.shape[0]
    blk_id = lax.broadcasted_iota(jnp.int32, gate.shape, 0)
    lane = lax.broadcasted_iota(jnp.int32, (ATT_HEADS, LANES), 1)
    out = jnp.zeros((ATT_HEADS, LANES), jnp.int32)
    for r in range(MOBA_TOPK):
        m = jnp.max(gate, axis=0, keepdims=True)
        idx = jnp.min(jnp.where(gate == m, blk_id, n_blocks), axis=0, keepdims=True)
        gate = jnp.where(blk_id == idx, -jnp.inf, gate)
        out = jnp.where(lane == r, idx[0], out)
    idx_ref[0] = out


def _top_blocks(q3, kmean_parts):
    n_req = q3.shape[0]
    return pl.pallas_call(
        _top_blocks_kernel,
        grid=(n_req,),
        in_specs=[pl.BlockSpec((1, ATT_HEADS, ATT_HEAD_DIM), lambda b: (b, 0, 0))]
        + [pl.BlockSpec((1,) + part.shape[1:], lambda b: (b, 0, 0, 0)) for part in kmean_parts],
        out_specs=pl.BlockSpec((1, ATT_HEADS, LANES), lambda b: (b, 0, 0)),
        out_shape=jax.ShapeDtypeStruct((n_req, ATT_HEADS, LANES), jnp.int32),
        compiler_params=_params("parallel"),
        name="moba_top_blocks",
    )(q3, *kmean_parts)


SEL_PAGES = MOBA_TOPK * PAGES_PER_BLOCK


def _moba_sample_kernel(idx_ref, pt_ref, q_ref, kn_ref, vn_ref, ck_ref, cv_ref, o_ref, kbuf, vbuf, sem, *, n_req, n_pages):
    b = pl.program_id(0)
    slot = b % 2

    def slab_copies(req, dst_slot):
        copies = []
        for h in range(ATT_HEADS):
            for sel in range(MOBA_TOPK):
                blk = idx_ref[(req * ATT_HEADS + h) * MOBA_TOPK + sel]
                for r in range(PAGES_PER_BLOCK):
                    page = pt_ref[req * n_pages + blk * PAGES_PER_BLOCK + r]
                    j = sel * PAGES_PER_BLOCK + r
                    copies.append(pltpu.make_async_copy(ck_ref.at[page, :, h, :], kbuf.at[dst_slot, h, j], sem.at[0, dst_slot]))
                    copies.append(pltpu.make_async_copy(cv_ref.at[page, :, h, :], vbuf.at[dst_slot, h, j], sem.at[1, dst_slot]))
        return copies

    @pl.when(b == 0)
    def _():
        for cp in slab_copies(0, 0):
            cp.start()

    @pl.when(b + 1 < n_req)
    def _():
        for cp in slab_copies(b + 1, 1 - slot):
            cp.start()

    for cp in slab_copies(b, slot):
        cp.wait()

    scale = ATT_HEAD_DIM ** -0.5
    q_all = q_ref[0]
    kn_all = kn_ref[0]
    vn_all = vn_ref[0]
    for h in range(ATT_HEADS):
        q = q_all[h:h + 1, :]
        q8 = jnp.broadcast_to(q, (SUBLANES, ATT_HEAD_DIM)).astype(BF16)
        s_own = jnp.sum(q * kn_all[h:h + 1, :], axis=-1, keepdims=True) * scale
        scores = [lax.dot_general(q8, kbuf[slot, h, j].astype(BF16), NT_DIMS, preferred_element_type=F32) * scale
                  for j in range(SEL_PAGES)]
        m = s_own
        for s in scores:
            m = jnp.maximum(m, jnp.max(s, axis=-1, keepdims=True))
        p_own = jnp.exp(s_own - m)
        l = p_own
        acc = p_own * vn_all[h:h + 1, :]
        for j, s in enumerate(scores):
            p = jnp.exp(s - m)
            l = l + jnp.sum(p, axis=-1, keepdims=True)
            acc = acc + jnp.dot(p.astype(BF16), vbuf[slot, h, j].astype(BF16), preferred_element_type=F32)
        o_ref[0, h:h + 1, :] = (acc / l)[0:1, :]


def _moba_sample(q3, k3, v3, cache_k, cache_v, idx_flat, page_table_flat, n_req, n_pages):
    row_spec = pl.BlockSpec((1, ATT_HEADS, ATT_HEAD_DIM), lambda b, idx, pt: (b, 0, 0))
    any_spec = pl.BlockSpec(memory_space=pl.ANY)
    slab_buf = pltpu.VMEM((2, ATT_HEADS, SEL_PAGES, PAGE_SIZE, ATT_HEAD_DIM), F32)
    return pl.pallas_call(
        functools.partial(_moba_sample_kernel, n_req=n_req, n_pages=n_pages),
        grid_spec=pltpu.PrefetchScalarGridSpec(
            num_scalar_prefetch=2,
            grid=(n_req,),
            in_specs=[row_spec, row_spec, row_spec, any_spec, any_spec],
            out_specs=row_spec,
            scratch_shapes=[slab_buf, slab_buf, pltpu.SemaphoreType.DMA((2, 2))],
        ),
        out_shape=jax.ShapeDtypeStruct(q3.shape, F32),
        compiler_params=_params("arbitrary"),
        name="moba_sample",
    )(idx_flat, page_table_flat, q3, k3, v3, cache_k, cache_v)


def _pair_block_diag(w):
    hds, d, _ = w.shape
    wp = w.reshape(hds // 2, 2, d, d)
    z = jnp.zeros((hds // 2, d, d), w.dtype)
    top = jnp.concatenate([wp[:, 0], z], axis=-1)
    bot = jnp.concatenate([z, wp[:, 1]], axis=-1)
    return jnp.concatenate([top, bot], axis=-2)


def _lane_pad(vec, offset):
    out = jnp.zeros((1, LANES), F32)
    return out.at[0, offset:offset + vec.shape[0]].set(vec.astype(F32))


def kernel(x_prompt, x_sample, state_conv, state_rglru_h, state_gdn, cache_k, cache_v, page_table, norm_mix, norm_ffn, norm_final, w_in0, conv0_w, conv0_b, rg_wa, rg_ba, rg_wi, rg_bi, rg_lambda, gdn_a_log, gdn_dt_bias, gdn_norm, w_out0, w_qkv1, w_out1, moe_w_group, moe_b_group, moe_w_router, moe_b_router, moe_w_gate, moe_w_up, moe_w_down):
    bp, tp, d = x_prompt.shape
    bs, ts, _ = x_sample.shape
    n_pages = page_table.shape[1]
    assert ts == 1 and d == D_MODEL
    assert n_pages % PAGES_PER_BLOCK == 0
    assert n_pages // PAGES_PER_BLOCK >= MOBA_TOPK

    w_in0_b = jnp.pad(w_in0, ((0, 0), (0, IN0_PAD - IN0_DIM))).astype(BF16)
    w_out0_b = w_out0.astype(BF16)
    w_qkv1_b = w_qkv1.astype(BF16)
    w_out1_b = w_out1.astype(BF16)
    wg_b, wu_b, wd_b = moe_w_gate.astype(BF16), moe_w_up.astype(BF16), moe_w_down.astype(BF16)
    mix_w = (conv0_w, conv0_b.reshape(1, CONV_CH),
             _pair_block_diag(rg_wa).astype(BF16), rg_ba.reshape(1, RG_WIDTH),
             _pair_block_diag(rg_wi).astype(BF16), rg_bi.reshape(1, RG_WIDTH),
             rg_lambda.reshape(1, RG_WIDTH), _lane_pad(gdn_a_log, GDN_HEADS), _lane_pad(gdn_dt_bias, GDN_HEADS),
             gdn_norm.reshape(1, GDN_DV), w_out0_b)
    router_w = [jnp.pad(jnp.concatenate([moe_w_router[l], moe_w_group[l]], axis=-1),
                        ((0, 0), (0, LANES - MOE_EXPERTS - MOE_GROUPS))) for l in range(2)]
    router_b = [_lane_pad(jnp.concatenate([moe_b_router[l], moe_b_group[l]]), 0) for l in range(2)]

    xp = x_prompt.reshape(bp * tp, d)
    xs8 = jnp.pad(x_sample, ((0, 0), (0, SUBLANES - ts), (0, 0))).reshape(bs * SUBLANES, d)

    (proj_p,) = _norm_matmul(xp, norm_mix[0], w_in0_b, (IN0_PAD,), tm=512)
    (proj_s,) = _norm_matmul(xs8, norm_mix[0], w_in0_b, (IN0_PAD,), tm=bs * SUBLANES)
    xp, p_h, p_gdn = _mixer(proj_p, xp,
                            jnp.zeros((bp, SUBLANES, CONV_CH), F32), jnp.zeros((bp, 1, RG_WIDTH), F32),
                            jnp.zeros((bp, GDN_HEADS, GDN_DK, GDN_DV), F32), mix_w,
                            n_seq=bp, t_len=tp, tc=256, chunk=GDN_CHUNK, t_valid=tp)
    cbuf_s = jnp.pad(state_conv, ((0, 0), (SUBLANES - (CONV_W - 1), 0), (0, 0)))
    xs8, s_h, s_gdn = _mixer(proj_s, xs8, cbuf_s, state_rglru_h.reshape(bs, 1, RG_WIDTH), state_gdn, mix_w,
                             n_seq=bs, t_len=SUBLANES, tc=SUBLANES, chunk=SUBLANES, t_valid=ts)
    xs = xs8.reshape(bs, SUBLANES, d)[:, 0]
    p_conv = proj_p.reshape(bp, tp, IN0_PAD)[:, tp - (CONV_W - 1):, :CONV_CH]
    s_conv = jnp.concatenate([state_conv[:, ts:], proj_s.reshape(bs, SUBLANES, IN0_PAD)[:, :ts, :CONV_CH]], axis=1)

    moe0 = (norm_ffn[0], router_w[0], router_b[0], wg_b[0], wu_b[0], wd_b[0], norm_final)
    pt_flat = page_table.reshape(-1)
    half = (n_pages // (2 * PAGES_PER_BLOCK)) * PAGES_PER_BLOCK
    xp, kmean_lo = _sorted_moe(xp, *moe0, final_norm=False, cache=(cache_k, pt_flat, bs, n_pages, 0, half))
    xs = _hier_moe(xs, *moe0, tm=bs, final_norm=False)

    hd_all = ATT_HEADS * ATT_HEAD_DIM
    moe1 = (norm_ffn[1], router_w[1], router_b[1], wg_b[1], wu_b[1], wd_b[1], norm_final)
    qp, kp, vp = _norm_matmul(xp, norm_mix[1], w_qkv1_b, (hd_all,) * 3, tm=512)
    op = _moba_prompt(qp, kp, vp, bp, tp)
    xp = _matmul_residual(op, w_out1_b, xp, tm=512)
    y_p, kmean_hi = _sorted_moe(xp, *moe1, final_norm=True,
                                cache=(cache_k, pt_flat, bs, n_pages, half, n_pages - half))

    qs, ks, vs = _norm_matmul(xs, norm_mix[1], w_qkv1_b, (hd_all,) * 3, tm=bs)
    row3 = lambda a: a.reshape(bs, ATT_HEADS, ATT_HEAD_DIM)
    idx = _top_blocks(row3(qs), (kmean_lo, kmean_hi))
    os_ = _moba_sample(row3(qs), row3(ks), row3(vs), cache_k, cache_v,
                       idx[:, :, :MOBA_TOPK].reshape(-1), pt_flat, bs, n_pages)
    xs = _matmul_residual(os_.reshape(bs, hd_all), w_out1_b, xs, tm=bs)
    y_s = _hier_moe(xs, *moe1, tm=bs, final_norm=True)

    heads = lambda a, b_, t_: a.reshape(b_, t_, ATT_HEADS, ATT_HEAD_DIM)
    return (y_p.reshape(bp, tp, d), y_s.reshape(bs, ts, d),
            p_conv, p_h.reshape(bp, RG_WIDTH), p_gdn,
            heads(kp, bp, tp), heads(vp, bp, tp),
            s_conv, s_h.reshape(bs, RG_WIDTH), s_gdn,
            heads(ks, bs, ts), heads(vs, bs, ts))
```

```python
import functools

import jax
import jax.numpy as jnp
from jax import lax
from jax.experimental import pallas as pl
from jax.experimental.pallas import tpu as pltpu

F32 = jnp.float32
BF16 = jnp.bfloat16
HIGHEST = lax.Precision.HIGHEST

D_MODEL = 1024
RG_WIDTH = 512
RG_HEADS = 8
RG_HEAD_DIM = 64
RG_C = 8.0
GDN_HEADS = 4
GDN_DK = 128
GDN_DV = 128
GDN_CHUNK = 64
CONV_W = 4
CONV_CH = 2048
IN0_DIM = 3080
IN0_PAD = 3200
ATT_HEADS = 8
ATT_HEAD_DIM = 128
MOBA_BLOCK = 256
MOBA_TOPK = 3
PAGE_SIZE = 128
PAGES_PER_BLOCK = MOBA_BLOCK // PAGE_SIZE
MOE_GROUPS = 4
MOE_EXPERTS_PER_GROUP = 4
MOE_EXPERTS = 16
MOE_FF = 512
RMS_EPS = 1e-6
L2_EPS = 1e-6
NEG_INF = -1e30
LOG2_E = 1.4426950408889634

LANES = 128
SUBLANES = 8
VMEM_LIMIT = 56 * 1024 * 1024
MOE_TM = 1024
NT_DIMS = (((1,), (1,)), ((), ()))
TN_DIMS = (((0,), (0,)), ((), ()))


def _params(*sem):
    return pltpu.CompilerParams(dimension_semantics=sem, vmem_limit_bytes=VMEM_LIMIT)


def _rms(x, g):
    return x * lax.rsqrt(jnp.mean(x * x, axis=-1, keepdims=True) + RMS_EPS) * g


def _softplus(x):
    return jnp.maximum(x, 0.0) + jnp.log1p(jnp.exp(-jnp.abs(x)))


def _sigmoid(x):
    return 1.0 / (1.0 + jnp.exp(-x))


def _norm_matmul_kernel(x_ref, g_ref, w_ref, *out_refs, widths):
    h = _rms(x_ref[...], g_ref[...]).astype(BF16)
    lo = 0
    for o_ref, width in zip(out_refs, widths):
        o_ref[...] = jnp.dot(h, w_ref[:, lo:lo + width], preferred_element_type=F32)
        lo += width


def _norm_matmul(x, g, w_bf16, widths, tm):
    n, d = x.shape
    assert n % tm == 0 and sum(widths) == w_bf16.shape[1]
    return pl.pallas_call(
        functools.partial(_norm_matmul_kernel, widths=widths),
        grid=(n // tm,),
        in_specs=[pl.BlockSpec((tm, d), lambda i: (i, 0)),
                  pl.BlockSpec((1, d), lambda i: (0, 0)),
                  pl.BlockSpec(w_bf16.shape, lambda i: (0, 0))],
        out_specs=[pl.BlockSpec((tm, wd), lambda i: (i, 0)) for wd in widths],
        out_shape=[jax.ShapeDtypeStruct((n, wd), F32) for wd in widths],
        compiler_params=_params("parallel"),
        name="norm_matmul",
    )(x, g.reshape(1, d), w_bf16)


def _matmul_residual_kernel(a_ref, w_ref, r_ref, o_ref):
    o_ref[...] = r_ref[...] + jnp.dot(a_ref[...].astype(BF16), w_ref[...], preferred_element_type=F32)


def _matmul_residual(a, w_bf16, res, tm):
    n, k = a.shape
    m = w_bf16.shape[1]
    assert n % tm == 0
    return pl.pallas_call(
        _matmul_residual_kernel,
        grid=(n // tm,),
        in_specs=[pl.BlockSpec((tm, k), lambda i: (i, 0)),
                  pl.BlockSpec((k, m), lambda i: (0, 0)),
                  pl.BlockSpec((tm, m), lambda i: (i, 0))],
        out_specs=pl.BlockSpec((tm, m), lambda i: (i, 0)),
        out_shape=jax.ShapeDtypeStruct((n, m), F32),
        compiler_params=_params("parallel"),
        name="matmul_residual",
    )(a, w_bf16, res)


def _linear_scan(a, u, n_rows):
    row = lax.broadcasted_iota(jnp.int32, a.shape, 0)
    d = 1
    while d < n_rows:
        a_prev = jnp.where(row >= d, pltpu.roll(a, d, 0), 1.0)
        u_prev = jnp.where(row >= d, pltpu.roll(u, d, 0), 0.0)
        u = a * u_prev + u
        a = a * a_prev
        d *= 2
    return a, u


SPLIT_POWER_MAX = 2


def _bmm(a, b):
    return jnp.einsum('bij,bjk->bik', a.astype(BF16), b.astype(BF16), preferred_element_type=F32)


def _split_bf16(x):
    hi = x.astype(BF16)
    return hi, x - hi.astype(F32)


def _bmm_split(a, b):
    a_hi, a_lo = _split_bf16(a)
    b_hi, b_lo = _split_bf16(b)
    return _bmm(a_hi, b_hi) + (_bmm(a_hi, b_lo) + _bmm(a_lo, b_hi))


def _unit_lower_solve(lmat, rhs, c):
    m = -lmat
    sol = rhs + _bmm_split(m, rhs)
    p = 2
    while p < c:
        mm = _bmm_split if p <= SPLIT_POWER_MAX else _bmm
        m = mm(m, m)
        sol = sol + mm(m, sol)
        p *= 2
    return sol


N_MIXER_INPUTS = 19


def _mixer_kernel(*refs, tc, chunk, t_valid, nt, n_cache_pages):
    if n_cache_pages:
        refs = refs[1:]
    (conv_ref, gate_ref, z_ref, ba_ref, x_ref, cbuf0_ref, h0_ref, s0_ref,
     cw_ref, cb_ref, wa_ref, rba_ref, wi_ref, rbi_ref, lam_ref, alog_ref, dtb_ref, gnorm_ref, wout_ref) = refs[:N_MIXER_INPUTS]
    page_refs = refs[N_MIXER_INPUTS:N_MIXER_INPUTS + n_cache_pages]
    x1_ref, ht_ref, st_ref = refs[N_MIXER_INPUTS + n_cache_pages:N_MIXER_INPUTS + n_cache_pages + 3]
    refs = refs[N_MIXER_INPUTS + n_cache_pages + 3:]
    if n_cache_pages:
        _block_means(page_refs, refs[0])
        refs = refs[1:]
    cbuf, hcar, state, act = refs
    t = pl.program_id(1)

    @pl.when(t == 0)
    def _():
        cbuf[0:SUBLANES, :] = cbuf0_ref[0]
        hcar[...] = h0_ref[0]
        state[...] = s0_ref[0]

    cbuf[SUBLANES:SUBLANES + tc, :] = conv_ref[...]
    base = SUBLANES - (CONV_W - 1)
    y = cb_ref[...] + cbuf[base:base + tc, :] * cw_ref[0:1, :]
    for j in range(1, CONV_W):
        y = y + cbuf[base + j:base + j + tc, :] * cw_ref[j:j + 1, :]
    cbuf[0:SUBLANES, :] = cbuf[tc:tc + SUBLANES, :]

    masked = t_valid < nt * tc
    if masked:
        valid = (t * tc + lax.broadcasted_iota(jnp.int32, (tc, 1), 0)) < t_valid

    xr = y[:, :RG_WIDTH]
    xr_b = xr.astype(BF16)
    pair = 2 * RG_HEAD_DIM
    r_lin = jnp.concatenate(
        [jnp.dot(xr_b[:, pair * i:pair * (i + 1)], wa_ref[i], preferred_element_type=F32) for i in range(RG_WIDTH // pair)], axis=-1)
    i_lin = jnp.concatenate(
        [jnp.dot(xr_b[:, pair * i:pair * (i + 1)], wi_ref[i], preferred_element_type=F32) for i in range(RG_WIDTH // pair)], axis=-1)
    r = _sigmoid(r_lin + rba_ref[...])
    gi = _sigmoid(i_lin + rbi_ref[...])
    log_a = -RG_C * r * _softplus(-lam_ref[...])
    a = jnp.exp(log_a)
    u = jnp.sqrt(-jnp.tanh(log_a) * (a * a + 1.0)) * (gi * xr)
    if masked:
        a = jnp.where(valid, a, 1.0)
        u = jnp.where(valid, u, 0.0)
    a_cum, h_loc = _linear_scan(a, u, tc)
    h = h_loc + a_cum * hcar[...]
    hcar[...] = h[tc - 1:tc, :]
    act[:, 0:RG_WIDTH] = (h * jax.nn.gelu(gate_ref[...])).astype(BF16)

    qkv = y[:, RG_WIDTH:]
    qkv = qkv * _sigmoid(qkv)
    ba = ba_ref[...]
    beta_all = _sigmoid(ba)
    g_all = -jnp.exp(alog_ref[...]) * _softplus(ba + dtb_ref[...])
    if masked:
        beta_all = jnp.where(valid, beta_all, 0.0)
        g_all = jnp.where(valid, g_all, 0.0)
    in_chunk = lax.broadcasted_iota(jnp.int32, g_all.shape, 0) & (chunk - 1)
    gcum_all = g_all
    d = 1
    while d < chunk:
        gcum_all = gcum_all + jnp.where(in_chunk >= d, pltpu.roll(gcum_all, d, 0), 0.0)
        d *= 2

    nc = tc // chunk

    def chunk_head_batch(cols_of_head):
        per_head = [cols_of_head(hd) for hd in range(GDN_HEADS)]
        return jnp.concatenate([per_head[hd][ci * chunk:(ci + 1) * chunk][None]
                                for ci in range(nc) for hd in range(GDN_HEADS)], axis=0)

    def l2_normalised(x):
        return x * lax.rsqrt(jnp.sum(x * x, axis=-1, keepdims=True) + L2_EPS)

    q_b = chunk_head_batch(lambda hd: l2_normalised(qkv[:, GDN_DK * hd:GDN_DK * (hd + 1)]) * (GDN_DK ** -0.5))
    k_b = chunk_head_batch(lambda hd: l2_normalised(qkv[:, 512 + GDN_DK * hd:512 + GDN_DK * (hd + 1)]))
    v_b = chunk_head_batch(lambda hd: qkv[:, 1024 + GDN_DV * hd:1024 + GDN_DV * (hd + 1)])
    beta_b = chunk_head_batch(lambda hd: beta_all[:, hd:hd + 1])
    gcol = chunk_head_batch(lambda hd: gcum_all[:, GDN_HEADS + hd:GDN_HEADS + hd + 1])

    nbatch = nc * GDN_HEADS
    crow = lax.broadcasted_iota(jnp.int32, (nbatch, chunk, chunk), 1)
    ccol = lax.broadcasted_iota(jnp.int32, (nbatch, chunk, chunk), 2)
    lower = crow >= ccol
    strict = crow > ccol
    grow = jnp.sum(jnp.where(crow == ccol, gcol, 0.0), axis=1, keepdims=True)
    decay = jnp.where(lower, jnp.exp(jnp.where(lower, gcol - grow, 0.0)), 0.0)
    kb = k_b * beta_b
    k_bf = k_b.astype(BF16)
    kk = jnp.einsum('bik,bjk->bij', kb.astype(BF16), k_bf, preferred_element_type=F32)
    rhs = jnp.concatenate([v_b * beta_b, kb * jnp.exp(gcol)], axis=-1)
    sol = _unit_lower_solve(jnp.where(strict, kk * decay, 0.0), rhs, chunk)
    qk = jnp.where(lower, jnp.einsum('bik,bjk->bij', q_b.astype(BF16), k_bf, preferred_element_type=F32) * decay, 0.0)
    qk_bf = qk.astype(BF16)
    q_dec = (q_b * jnp.exp(gcol)).astype(BF16)
    g_last = gcol[:, chunk - 1:chunk, :]
    k_dec = (k_b * jnp.exp(g_last - gcol)).astype(BF16)
    s_decay = jnp.exp(g_last)
    u_all = sol[:, :, :GDN_DV]
    w_bf = sol[:, :, GDN_DV:].astype(BF16)

    gnorm = gnorm_ref[...]
    z_all = z_ref[...]
    s_heads = [state[hd] for hd in range(GDN_HEADS)]
    for ci in range(nc):
        lo = ci * chunk
        for hd in range(GDN_HEADS):
            bi = ci * GDN_HEADS + hd
            s_h = s_heads[hd]
            s_bf = s_h.astype(BF16)
            v_new = u_all[bi] - jnp.dot(w_bf[bi], s_bf, preferred_element_type=F32)
            v_new_bf = v_new.astype(BF16)
            o = (jnp.dot(q_dec[bi], s_bf, preferred_element_type=F32)
                 + jnp.dot(qk_bf[bi], v_new_bf, preferred_element_type=F32))
            s_heads[hd] = s_h * s_decay[bi] + lax.dot_general(k_dec[bi], v_new_bf, TN_DIMS, preferred_element_type=F32)
            zh = z_all[lo:lo + chunk, GDN_DV * hd:GDN_DV * (hd + 1)]
            o = _rms(o, gnorm) * (zh * _sigmoid(zh))
            act[lo:lo + chunk, RG_WIDTH + GDN_DV * hd:RG_WIDTH + GDN_DV * (hd + 1)] = o.astype(BF16)
    for hd in range(GDN_HEADS):
        state[hd] = s_heads[hd]

    x1_ref[...] = x_ref[...] + jnp.dot(act[...], wout_ref[...], preferred_element_type=F32)

    @pl.when(t == nt - 1)
    def _():
        ht_ref[0] = hcar[...]
        st_ref[0] = state[...]


def _mixer(proj, x, cbuf0, h0, s0, wts, *, n_seq, t_len, tc, chunk, t_valid, cache=None):
    nt = t_len // tc
    assert t_len % tc == 0 and tc % chunk == 0 and tc % SUBLANES == 0
    row = lambda b, t: b * nt + t
    const2 = lambda b, t, *_: (0, 0)
    const3 = lambda b, t, *_: (0, 0, 0)
    (cw, cb, wa4, rba, wi4, rbi, lam, alog, dtb, gnorm, wout) = wts
    in_specs = [
        pl.BlockSpec((tc, CONV_CH), lambda b, t, *_: (row(b, t), 0)),
        pl.BlockSpec((tc, RG_WIDTH), lambda b, t, *_: (row(b, t), CONV_CH // RG_WIDTH)),
        pl.BlockSpec((tc, RG_WIDTH), lambda b, t, *_: (row(b, t), CONV_CH // RG_WIDTH + 1)),
        pl.BlockSpec((tc, LANES), lambda b, t, *_: (row(b, t), (CONV_CH + 2 * RG_WIDTH) // LANES)),
        pl.BlockSpec((tc, D_MODEL), lambda b, t, *_: (row(b, t), 0)),
        pl.BlockSpec((1, SUBLANES, CONV_CH), lambda b, t, *_: (b, 0, 0)),
        pl.BlockSpec((1, 1, RG_WIDTH), lambda b, t, *_: (b, 0, 0)),
        pl.BlockSpec((1, GDN_HEADS, GDN_DK, GDN_DV), lambda b, t, *_: (b, 0, 0, 0)),
        pl.BlockSpec(cw.shape, const2), pl.BlockSpec(cb.shape, const2),
        pl.BlockSpec(wa4.shape, const3), pl.BlockSpec(rba.shape, const2),
        pl.BlockSpec(wi4.shape, const3), pl.BlockSpec(rbi.shape, const2),
        pl.BlockSpec(lam.shape, const2), pl.BlockSpec(alog.shape, const2), pl.BlockSpec(dtb.shape, const2),
        pl.BlockSpec(gnorm.shape, const2), pl.BlockSpec(wout.shape, const2),
    ]
    out_specs = [
        pl.BlockSpec((tc, D_MODEL), lambda b, t, *_: (row(b, t), 0)),
        pl.BlockSpec((1, 1, RG_WIDTH), lambda b, t, *_: (b, 0, 0)),
        pl.BlockSpec((1, GDN_HEADS, GDN_DK, GDN_DV), lambda b, t, *_: (b, 0, 0, 0)),
    ]
    out_shape = [
        jax.ShapeDtypeStruct((n_seq * t_len, D_MODEL), F32),
        jax.ShapeDtypeStruct((n_seq, 1, RG_WIDTH), F32),
        jax.ShapeDtypeStruct((n_seq, GDN_HEADS, GDN_DK, GDN_DV), F32),
    ]
    args = [proj, proj, proj, proj, x, cbuf0, h0, s0, cw, cb, wa4, rba, wi4, rbi, lam, alog, dtb, gnorm, wout]
    assert len(args) == N_MIXER_INPUTS
    prefetch = []
    n_cache_pages = 0
    if cache is not None:
        n_cache_pages = HOST_CACHE_PAGES_PER_STEP
        prefetch = [cache[1]]
        page_specs, page_args, km_spec, km_shape = _cache_stream(cache, n_cache_pages, 2, row, n_seq * nt)
        in_specs += page_specs
        args += page_args
        out_specs.append(km_spec)
        out_shape.append(km_shape)
    return pl.pallas_call(
        functools.partial(_mixer_kernel, tc=tc, chunk=chunk, t_valid=t_valid, nt=nt, n_cache_pages=n_cache_pages),
        grid_spec=pltpu.PrefetchScalarGridSpec(
            num_scalar_prefetch=len(prefetch), grid=(n_seq, nt), in_specs=in_specs, out_specs=out_specs,
            scratch_shapes=[pltpu.VMEM((tc + SUBLANES, CONV_CH), F32),
                            pltpu.VMEM((1, RG_WIDTH), F32),
                            pltpu.VMEM((GDN_HEADS, GDN_DK, GDN_DV), F32),
                            pltpu.VMEM((tc, D_MODEL), BF16)]),
        out_shape=out_shape,
        compiler_params=_params("arbitrary", "arbitrary"),
        name="mixer0",
    )(*prefetch, *args)


def _route(h, w_ref, b_ref):
    logits = jnp.dot(h, w_ref[...], precision=HIGHEST, preferred_element_type=F32) + b_ref[...]
    lane = lax.broadcasted_iota(jnp.int32, logits.shape, 1)
    big = jnp.int32(1 << 20)
    is_grp = (lane >= MOE_EXPERTS) & (lane < MOE_EXPERTS + MOE_GROUPS)
    gl = jnp.where(is_grp, logits, -jnp.inf)
    gmax = jnp.max(gl, axis=-1, keepdims=True)
    gidx = jnp.min(jnp.where(gl == gmax, lane, big), axis=-1, keepdims=True) - MOE_EXPERTS
    g_top = 1.0 / jnp.sum(jnp.exp(gl - gmax), axis=-1, keepdims=True)
    in_grp = (lane >= gidx * MOE_EXPERTS_PER_GROUP) & (lane < (gidx + 1) * MOE_EXPERTS_PER_GROUP)
    el = jnp.where(in_grp, logits, -jnp.inf)
    m1 = jnp.max(el, axis=-1, keepdims=True)
    i1 = jnp.min(jnp.where(el == m1, lane, big), axis=-1, keepdims=True)
    el2 = jnp.where(lane == i1, -jnp.inf, el)
    m2 = jnp.max(el2, axis=-1, keepdims=True)
    i2 = jnp.min(jnp.where(el2 == m2, lane, big), axis=-1, keepdims=True)
    e2 = jnp.exp(m2 - m1)
    w1 = g_top / (1.0 + e2)
    w2 = g_top * e2 / (1.0 + e2)
    return jnp.where(lane == i1, w1, 0.0) + jnp.where(lane == i2, w2, 0.0), gidx


def _router_kernel(x_ref, g_ref, w_ref, b_ref, h_ref, gates_ref):
    h = _rms(x_ref[...], g_ref[...])
    h_ref[...] = h.astype(BF16)
    gates_ref[...], _ = _route(h, w_ref, b_ref)


def _moe_router(x, g, w_cat, b_cat, tm):
    n, d = x.shape
    return pl.pallas_call(
        _router_kernel,
        grid=(n // tm,),
        in_specs=[pl.BlockSpec((tm, d), lambda i: (i, 0)),
                  pl.BlockSpec((1, d), lambda i: (0, 0)),
                  pl.BlockSpec((d, LANES), lambda i: (0, 0)),
                  pl.BlockSpec((1, LANES), lambda i: (0, 0))],
        out_specs=[pl.BlockSpec((tm, d), lambda i: (i, 0)),
                   pl.BlockSpec((tm, LANES), lambda i: (i, 0))],
        out_shape=[jax.ShapeDtypeStruct((n, d), BF16), jax.ShapeDtypeStruct((n, LANES), F32)],
        compiler_params=_params("parallel"),
        name="moe_router",
    )(x, g.reshape(1, d), w_cat, b_cat)


def _cache_stream(cache, pages_per_step, n_grid_dims, step_of, n_steps):
    cache_k, pt_flat, n_req, n_pages, first_page, span = cache
    assert span % pages_per_step == 0 and pages_per_step % PAGES_PER_BLOCK == 0 and first_page % PAGES_PER_BLOCK == 0
    groups = span // pages_per_step
    stream_steps = n_req * groups
    assert stream_steps <= n_steps

    def group_of(ids):
        s = jnp.minimum(step_of(*ids), stream_steps - 1)
        return s // groups, s % groups

    def page_index(*args, r):
        req, grp = group_of(args[:n_grid_dims])
        return (args[-1][req * n_pages + first_page + grp * pages_per_step + r], 0, 0, 0)

    def out_index(*args):
        req, grp = group_of(args[:n_grid_dims])
        return (req, grp, 0, 0)

    in_specs = [pl.BlockSpec((1, PAGE_SIZE, ATT_HEADS, ATT_HEAD_DIM), functools.partial(page_index, r=r))
                for r in range(pages_per_step)]
    out_spec = pl.BlockSpec((1, pages_per_step // PAGES_PER_BLOCK, ATT_HEADS, ATT_HEAD_DIM), out_index)
    out_shape = jax.ShapeDtypeStruct((n_req, span // PAGES_PER_BLOCK, ATT_HEADS, ATT_HEAD_DIM), F32)
    return in_specs, [cache_k] * pages_per_step, out_spec, out_shape


def _block_means(page_refs, o_ref):
    for b in range(len(page_refs) // PAGES_PER_BLOCK):
        s = jnp.sum(page_refs[PAGES_PER_BLOCK * b][0], axis=0)
        for r in range(1, PAGES_PER_BLOCK):
            s = s + jnp.sum(page_refs[PAGES_PER_BLOCK * b + r][0], axis=0)
        o_ref[0, b] = s / MOBA_BLOCK


def _moe_expert_kernel(*refs, final_norm, n_cache_pages):
    if n_cache_pages:
        refs = refs[1:]
    h_ref, gates_ref, x_ref, wg_ref, wu_ref, wd_ref, gf_ref = refs[:7]
    page_refs = refs[7:7 + n_cache_pages]
    o_ref = refs[7 + n_cache_pages]
    e = pl.program_id(1)

    if n_cache_pages:
        _block_means(page_refs, refs[8 + n_cache_pages])

    @pl.when(e == 0)
    def _():
        o_ref[...] = x_ref[...]

    h = h_ref[...]
    a = jnp.dot(h, wg_ref[0], preferred_element_type=F32)
    b = jnp.dot(h, wu_ref[0], preferred_element_type=F32)
    gates = gates_ref[...]
    lane = lax.broadcasted_iota(jnp.int32, gates.shape, 1)
    gcol = jnp.sum(jnp.where(lane == e, gates, 0.0), axis=-1, keepdims=True)
    hid = (a * _sigmoid(a)) * b * gcol
    o_ref[...] += jnp.dot(hid.astype(BF16), wd_ref[0], preferred_element_type=F32)

    if final_norm:
        @pl.when(e == MOE_EXPERTS - 1)
        def _():
            o_ref[...] = _rms(o_ref[...], gf_ref[...])


def _moe_experts(h, gates, x, wg, wu, wd, g_final, tm, final_norm, cache=None):
    n, d = x.shape
    grid = (n // tm, MOE_EXPERTS)
    in_specs = [pl.BlockSpec((tm, d), lambda i, e, *_: (i, 0)),
                pl.BlockSpec((tm, LANES), lambda i, e, *_: (i, 0)),
                pl.BlockSpec((tm, d), lambda i, e, *_: (i, 0)),
                pl.BlockSpec((1, d, MOE_FF), lambda i, e, *_: (e, 0, 0)),
                pl.BlockSpec((1, d, MOE_FF), lambda i, e, *_: (e, 0, 0)),
                pl.BlockSpec((1, MOE_FF, d), lambda i, e, *_: (e, 0, 0)),
                pl.BlockSpec((1, d), lambda i, e, *_: (0, 0))]
    out_specs = [pl.BlockSpec((tm, d), lambda i, e, *_: (i, 0))]
    out_shape = [jax.ShapeDtypeStruct((n, d), F32)]
    args = [h, gates, x, wg, wu, wd, g_final.reshape(1, d)]
    prefetch = []
    n_cache_pages = 0
    if cache is not None:
        cache_k, pt_flat, n_req, n_pages, first_page, span = cache
        steps = grid[0] * grid[1]
        assert (n_req * span) % steps == 0 and steps % n_req == 0
        n_cache_pages = n_req * span // steps
        groups = steps // n_req
        assert n_cache_pages % PAGES_PER_BLOCK == 0 and first_page % PAGES_PER_BLOCK == 0
        prefetch = [pt_flat]

        def page_index(i, e, pt, r):
            step = i * MOE_EXPERTS + e
            return (pt[(step // groups) * n_pages + first_page + (step % groups) * n_cache_pages + r], 0, 0, 0)

        for r in range(n_cache_pages):
            in_specs.append(pl.BlockSpec((1, PAGE_SIZE, ATT_HEADS, ATT_HEAD_DIM), functools.partial(page_index, r=r)))
        args += [cache_k] * n_cache_pages
        blocks_per_step = n_cache_pages // PAGES_PER_BLOCK
        out_specs.append(pl.BlockSpec(
            (1, blocks_per_step, ATT_HEADS, ATT_HEAD_DIM),
            lambda i, e, *_: ((i * MOE_EXPERTS + e) // groups, (i * MOE_EXPERTS + e) % groups, 0, 0)))
        out_shape.append(jax.ShapeDtypeStruct((n_req, span // PAGES_PER_BLOCK, ATT_HEADS, ATT_HEAD_DIM), F32))
    outs = pl.pallas_call(
        functools.partial(_moe_expert_kernel, final_norm=final_norm, n_cache_pages=n_cache_pages),
        grid_spec=pltpu.PrefetchScalarGridSpec(
            num_scalar_prefetch=len(prefetch), grid=grid, in_specs=in_specs, out_specs=out_specs),
        out_shape=out_shape,
        compiler_params=_params("parallel", "arbitrary"),
        name="moe_experts",
    )(*prefetch, *args)
    return outs if cache is not None else outs[0]


def _hier_moe(x, g_norm, w_cat, b_cat, wg, wu, wd, g_final, tm, final_norm, cache=None):
    h, gates = _moe_router(x, g_norm, w_cat, b_cat, tm)
    return _moe_experts(h, gates, x, wg, wu, wd, g_final, tm, final_norm, cache)


SORT_TILE = 256
GRANULE = 16
SORT_ROWS = SORT_TILE + MOE_GROUPS * GRANULE
GROUP_TILE = 512
GRANULES_PER_TILE = GROUP_TILE // GRANULE
MOE_CACHE_PAGES_PER_STEP = 8
HOST_CACHE_PAGES_PER_STEP = 16


def _dispatch_kernel(x_ref, g_ref, w_ref, b_ref, hs_ref, gs_ref, slot_ref, cnt_ref):
    h = _rms(x_ref[...], g_ref[...])
    gates, gidx = _route(h, w_ref, b_ref)
    r = h.shape[0]
    lane = lax.broadcasted_iota(jnp.int32, (r, LANES), 1)
    onehot = jnp.where(lane == gidx, 1.0, 0.0)
    tok = lax.broadcasted_iota(jnp.int32, (r, r), 0)
    other = lax.broadcasted_iota(jnp.int32, (r, r), 1)
    earlier = jnp.where(other < tok, 1.0, 0.0).astype(BF16)
    prefix = jnp.dot(earlier, onehot.astype(BF16), preferred_element_type=F32)
    rank = jnp.sum(prefix * onehot, axis=-1, keepdims=True)
    counts = jnp.sum(onehot, axis=0, keepdims=True)
    padded = jnp.floor((counts + (GRANULE - 1.0)) * (1.0 / GRANULE)) * GRANULE
    lane1 = lax.broadcasted_iota(jnp.int32, (1, LANES), 1)
    seg_start = jnp.zeros((1, LANES), F32)
    running = jnp.zeros((1, 1), F32)
    for grp in range(MOE_GROUPS):
        seg_start = jnp.where(lane1 == grp, running, seg_start)
        running = running + padded[:, grp:grp + 1]
    slot = jnp.sum(onehot * seg_start, axis=-1, keepdims=True) + rank
    slot_b = jnp.broadcast_to(slot, (r, LANES))
    slot_row = slot_b.T[0:1, :]
    srow = lax.broadcasted_iota(jnp.int32, (SORT_ROWS, r), 0).astype(F32)
    perm = jnp.where(srow == slot_row, 1.0, 0.0)
    hs_ref[...] = jnp.dot(perm.astype(BF16), h.astype(BF16), preferred_element_type=F32).astype(BF16)
    gs_ref[...] = jnp.dot(perm, gates, precision=HIGHEST, preferred_element_type=F32)
    slot_ref[...] = slot_b.astype(jnp.int32)
    cnt_ref[0] = jnp.broadcast_to(counts, (SUBLANES, LANES)).astype(jnp.int32)


def _moe_dispatch(x, g, w_cat, b_cat):
    n, d = x.shape
    nt = n // SORT_TILE
    return pl.pallas_call(
        _dispatch_kernel,
        grid=(nt,),
        in_specs=[pl.BlockSpec((SORT_TILE, d), lambda i: (i, 0)),
                  pl.BlockSpec((1, d), lambda i: (0, 0)),
                  pl.BlockSpec((d, LANES), lambda i: (0, 0)),
                  pl.BlockSpec((1, LANES), lambda i: (0, 0))],
        out_specs=[pl.BlockSpec((SORT_ROWS, d), lambda i: (i, 0)),
                   pl.BlockSpec((SORT_ROWS, LANES), lambda i: (i, 0)),
                   pl.BlockSpec((SORT_TILE, LANES), lambda i: (i, 0)),
                   pl.BlockSpec((1, SUBLANES, LANES), lambda i: (i, 0, 0))],
        out_shape=[jax.ShapeDtypeStruct((nt * SORT_ROWS, d), BF16),
                   jax.ShapeDtypeStruct((nt * SORT_ROWS, LANES), F32),
                   jax.ShapeDtypeStruct((n, LANES), jnp.int32),
                   jax.ShapeDtypeStruct((nt, SUBLANES, LANES), jnp.int32)],
        compiler_params=_params("parallel"),
        name="moe_dispatch",
    )(x, g.reshape(1, d), w_cat, b_cat)


def _group_tile_tables(cnt, n_group_tiles):
    nt, ng = cnt.shape
    pc = (cnt + GRANULE - 1) // GRANULE * GRANULE
    seg_row = (jnp.cumsum(pc, axis=1) - pc).T.reshape(-1)
    seg_tile = jnp.tile(jnp.arange(nt), ng)
    ngr = (pc // GRANULE).T.reshape(-1)
    seg_end = jnp.cumsum(ngr)
    seg_first = seg_end - ngr
    k = jnp.arange(nt * SORT_ROWS // GRANULE)[:, None]
    in_seg = (k >= seg_first[None, :]) & (k < seg_end[None, :])
    seg_base = seg_tile * SORT_ROWS + seg_row - seg_first * GRANULE
    gran_row = jnp.sum(jnp.where(in_seg, seg_base[None, :] + k * GRANULE, 0), axis=1)
    group_gran = jnp.sum(ngr.reshape(ng, nt), axis=1)
    group_first = jnp.cumsum(group_gran) - group_gran
    tiles = (group_gran + GRANULES_PER_TILE - 1) // GRANULES_PER_TILE
    tile_end = jnp.cumsum(tiles)
    tile_first = tile_end - tiles
    j = jnp.arange(n_group_tiles)[:, None]
    in_group = (j >= tile_first[None, :]) & (j < tile_end[None, :])
    pick = lambda per_group: jnp.sum(jnp.where(in_group, per_group[None, :], 0), axis=1)
    used = jnp.any(in_group, axis=1)
    tg = jnp.where(used, pick(jnp.arange(ng)), ng - 1)
    local = j[:, 0] - pick(tile_first)
    first = pick(group_first) + local * GRANULES_PER_TILE
    count = jnp.where(used, jnp.clip(pick(group_gran) - local * GRANULES_PER_TILE, 0, GRANULES_PER_TILE), 0)
    i32 = lambda a: a.astype(jnp.int32)
    return i32(gran_row), i32(tg), i32(jnp.where(used, first, 0)), i32(count)


def _moe_group_kernel(*refs, n_cache_pages):
    gran_ref, tgrp_ref, tfirst_ref, tcount_ref = refs[:4]
    refs = refs[5:] if n_cache_pages else refs[4:]
    wg_ref, wu_ref, wd_ref, hs_ref, gs_ref, _ = refs[:6]
    page_refs = refs[6:6 + n_cache_pages]
    y_ref = refs[6 + n_cache_pages]
    refs = refs[7 + n_cache_pages:]
    if n_cache_pages:
        _block_means(page_refs, refs[0])
        refs = refs[1:]
    hbuf, gbuf, ybuf, sem = refs
    j = pl.program_id(0)
    e = pl.program_id(1)
    count = tcount_ref[j]
    first = tfirst_ref[j]

    def buf_rows(i):
        return pl.ds(pl.multiple_of(i * GRANULE, GRANULE), GRANULE)

    def sorted_rows(i):
        return pl.ds(pl.multiple_of(gran_ref[first + i], GRANULE), GRANULE)

    def gather_copies(i):
        return (pltpu.make_async_copy(hs_ref.at[sorted_rows(i), :], hbuf.at[buf_rows(i), :], sem.at[0]),
                pltpu.make_async_copy(gs_ref.at[sorted_rows(i), :], gbuf.at[buf_rows(i), :], sem.at[1]))

    def scatter_copy(i):
        return pltpu.make_async_copy(ybuf.at[buf_rows(i), :], y_ref.at[sorted_rows(i), :], sem.at[2])

    def for_granules(lo, hi, fn):
        def body(i, carry):
            fn(i)
            return carry
        lax.fori_loop(lo, hi, body, 0)

    @pl.when(jnp.logical_and(e == 0, count > 0))
    def _():
        def start(i):
            for cp in gather_copies(i):
                cp.start()

        def wait(i):
            for cp in gather_copies(i):
                cp.wait()

        def clear(i):
            hbuf[buf_rows(i), :] = jnp.zeros((GRANULE, hbuf.shape[1]), hbuf.dtype)
            gbuf[buf_rows(i), :] = jnp.zeros((GRANULE, gbuf.shape[1]), gbuf.dtype)

        for_granules(0, count, start)
        for_granules(0, count, wait)
        for_granules(count, GRANULES_PER_TILE, clear)
        ybuf[...] = jnp.zeros_like(ybuf)

    @pl.when(count > 0)
    def _():
        h = hbuf[...]
        a = jnp.dot(h, wg_ref[0], preferred_element_type=F32)
        b = jnp.dot(h, wu_ref[0], preferred_element_type=F32)
        gates = gbuf[...]
        lane = lax.broadcasted_iota(jnp.int32, gates.shape, 1)
        expert = tgrp_ref[j] * MOE_EXPERTS_PER_GROUP + e
        gcol = jnp.sum(jnp.where(lane == expert, gates, 0.0), axis=-1, keepdims=True)
        hid = (a * _sigmoid(a)) * b * gcol
        ybuf[...] += jnp.dot(hid.astype(BF16), wd_ref[0], preferred_element_type=F32)

    @pl.when(jnp.logical_and(e == MOE_EXPERTS_PER_GROUP - 1, count > 0))
    def _():
        for_granules(0, count, lambda i: scatter_copy(i).start())
        for_granules(0, count, lambda i: scatter_copy(i).wait())


def _moe_group_experts(tables, hs, gs, wg, wu, wd, n_group_tiles, cache):
    rows, d = hs.shape
    grid = (n_group_tiles, MOE_EXPERTS_PER_GROUP)

    def weight_index(j, e, gran, tgrp, tfirst, tcount, *_):
        return (tgrp[j] * MOE_EXPERTS_PER_GROUP + jnp.where(tcount[j] > 0, e, MOE_EXPERTS_PER_GROUP - 1), 0, 0)

    any_spec = pl.BlockSpec(memory_space=pl.ANY)
    in_specs = [pl.BlockSpec((1, d, MOE_FF), weight_index),
                pl.BlockSpec((1, d, MOE_FF), weight_index),
                pl.BlockSpec((1, MOE_FF, d), weight_index),
                any_spec, any_spec, any_spec]
    prefetch = list(tables)
    args = [wg, wu, wd, hs, gs, jnp.zeros((rows, d), F32)]
    out_specs = [any_spec]
    out_shape = [jax.ShapeDtypeStruct((rows, d), F32)]
    n_cache_pages = 0
    if cache is not None:
        n_cache_pages = MOE_CACHE_PAGES_PER_STEP
        prefetch.append(cache[1])
        page_specs, page_args, km_spec, km_shape = _cache_stream(
            cache, n_cache_pages, 2, lambda j, e: j * MOE_EXPERTS_PER_GROUP + e, grid[0] * grid[1])
        in_specs += page_specs
        args += page_args
        out_specs.append(km_spec)
        out_shape.append(km_shape)
    zero_arg = len(prefetch) + 5
    outs = pl.pallas_call(
        functools.partial(_moe_group_kernel, n_cache_pages=n_cache_pages),
        grid_spec=pltpu.PrefetchScalarGridSpec(
            num_scalar_prefetch=len(prefetch), grid=grid, in_specs=in_specs, out_specs=out_specs,
            scratch_shapes=[pltpu.VMEM((GROUP_TILE, d), BF16), pltpu.VMEM((GROUP_TILE, LANES), F32),
                            pltpu.VMEM((GROUP_TILE, d), F32), pltpu.SemaphoreType.DMA((3,))]),
        out_shape=out_shape,
        input_output_aliases={zero_arg: 0},
        compiler_params=_params("arbitrary", "arbitrary"),
        name="moe_group_experts",
    )(*prefetch, *args)
    return outs if cache is not None else outs[0]


def _combine_kernel(x_ref, y_ref, slot_ref, gf_ref, o_ref, *, final_norm):
    slot = slot_ref[:, 0:1]
    col = lax.broadcasted_iota(jnp.int32, (x_ref.shape[0], SORT_ROWS), 1)
    unperm = jnp.where(col == slot, 1.0, 0.0)
    out = x_ref[...] + jnp.dot(unperm, y_ref[...], precision=HIGHEST, preferred_element_type=F32)
    if final_norm:
        out = _rms(out, gf_ref[...])
    o_ref[...] = out


def _moe_combine(x, y_sorted, slot, g_final, final_norm):
    n, d = x.shape
    return pl.pallas_call(
        functools.partial(_combine_kernel, final_norm=final_norm),
        grid=(n // SORT_TILE,),
        in_specs=[pl.BlockSpec((SORT_TILE, d), lambda i: (i, 0)),
                  pl.BlockSpec((SORT_ROWS, d), lambda i: (i, 0)),
                  pl.BlockSpec((SORT_TILE, LANES), lambda i: (i, 0)),
                  pl.BlockSpec((1, d), lambda i: (0, 0))],
        out_specs=pl.BlockSpec((SORT_TILE, d), lambda i: (i, 0)),
        out_shape=jax.ShapeDtypeStruct((n, d), F32),
        compiler_params=_params("parallel"),
        name="moe_combine",
    )(x, y_sorted, slot, g_final.reshape(1, d))


def _sorted_moe(x, g_norm, w_cat, b_cat, wg, wu, wd, g_final, final_norm, cache=None):
    n = x.shape[0]
    assert n % SORT_TILE == 0
    hs, gs, slot, cnt = _moe_dispatch(x, g_norm, w_cat, b_cat)
    n_granules = hs.shape[0] // GRANULE
    n_group_tiles = -(-n_granules // GRANULES_PER_TILE) + MOE_GROUPS
    tables = _group_tile_tables(cnt[:, 0, :MOE_GROUPS], n_group_tiles)
    outs = _moe_group_experts(tables, hs, gs, wg, wu, wd, n_group_tiles, cache)
    y_sorted = outs[0] if cache is not None else outs
    out = _moe_combine(x, y_sorted, slot, g_final, final_norm)
    return (out, outs[1]) if cache is not None else out


def _moba_prompt_kernel(*refs, t_len, n_cache_pages):
    if n_cache_pages:
        _block_means(refs[4:4 + n_cache_pages], refs[5 + n_cache_pages])
        refs = refs[1:4] + refs[4 + n_cache_pages:5 + n_cache_pages]
    q_ref, k_ref, v_ref, o_ref = refs
    nb = t_len // MOBA_BLOCK
    blk = MOBA_BLOCK
    c = (ATT_HEAD_DIM ** -0.5) * LOG2_E
    kf = k_ref[...]
    k_bf = kf.astype(BF16)
    q_t = q_ref[...].T
    q_t_bf = (q_t * c).astype(BF16)
    v_t_bf = v_ref[...].T.astype(BF16)
    kmean = jnp.mean(kf.reshape(nb, blk, ATT_HEAD_DIM), axis=1)
    key_pos =lax.broadcasted_iota(jnp.int32, (blk, blk), 0)
    query_pos = lax.broadcasted_iota(jnp.int32, (blk, blk), 1)
    causal = key_pos <= query_pos
    for i in range(nb):
        qs = slice(i * blk, (i + 1) * blk)
        sel = [None] * i
        if i > MOBA_TOPK:
            gate = jnp.dot(kmean, q_t[:, qs], precision=HIGHEST, preferred_element_type=F32)
            g = [gate[j:j + 1, :] for j in range(i)]
            rank = [jnp.zeros((1, blk), F32) for _ in range(i)]
            for lo_j in range(i):
                for hi_j in range(lo_j + 1, i):
                    lo_wins = jnp.where(g[lo_j] >= g[hi_j], 1.0, 0.0)
                    rank[hi_j] = rank[hi_j] + lo_wins
                    rank[lo_j] = rank[lo_j] + (1.0 - lo_wins)
            sel = [r < float(MOBA_TOPK) for r in rank]
        pieces = []
        s_max = None
        for j in range(i + 1):
            s = jnp.dot(k_bf[j * blk:(j + 1) * blk], q_t_bf[:, qs], preferred_element_type=F32)
            if j == i:
                s = jnp.where(causal, s, NEG_INF)
            elif sel[j] is not None:
                s = jnp.where(sel[j], s, NEG_INF)
            pieces.append(s)
            s_max = s if s_max is None else jnp.maximum(s_max, s)
        m = jnp.max(s_max, axis=0, keepdims=True)
        p_sum = jnp.zeros((blk, blk), F32)
        acc = jnp.zeros((ATT_HEAD_DIM, blk), F32)
        for j, s in enumerate(pieces):
            p = jnp.exp2(s - m)
            p_sum = p_sum + p
            acc = acc + jnp.dot(v_t_bf[:, j * blk:(j + 1) * blk], p.astype(BF16), preferred_element_type=F32)
        o_ref[qs, :] = (acc / jnp.sum(p_sum, axis=0, keepdims=True)).T


def _moba_prompt(q, k, v, n_seq, t_len, cache=None):
    assert t_len % MOBA_BLOCK == 0
    spec = pl.BlockSpec((t_len, ATT_HEAD_DIM), lambda b, h, *_: (b, h))
    in_specs, args, out_specs, out_shape = [spec, spec, spec], [q, k, v], [spec], [jax.ShapeDtypeStruct(q.shape, F32)]
    prefetch = []
    n_cache_pages = 0
    if cache is not None:
        n_cache_pages = HOST_CACHE_PAGES_PER_STEP
        prefetch = [cache[1]]
        page_specs, page_args, km_spec, km_shape = _cache_stream(
            cache, n_cache_pages, 2, lambda b, h: b * ATT_HEADS + h, n_seq * ATT_HEADS)
        in_specs += page_specs
        args += page_args
        out_specs.append(km_spec)
        out_shape.append(km_shape)
    outs = pl.pallas_call(
        functools.partial(_moba_prompt_kernel, t_len=t_len, n_cache_pages=n_cache_pages),
        grid_spec=pltpu.PrefetchScalarGridSpec(
            num_scalar_prefetch=len(prefetch), grid=(n_seq, ATT_HEADS), in_specs=in_specs, out_specs=out_specs),
        out_shape=out_shape,
        compiler_params=_params("parallel", "parallel"),
        name="moba_prompt",
    )(*prefetch, *args)
    return outs if cache is not None else outs[0]


def _top_blocks_kernel(q_ref, *refs):
    kmean_refs, idx_ref = refs[:-1], refs[-1]
    q = q_ref[0]
    kmean = jnp.concatenate([r[0] for r in kmean_refs], axis=0)
    gate = jnp.sum(kmean * q[None, :, :], axis=-1, keepdims=True)
    n_blocks = gate.shape[0]
    blk_id = lax.broadcasted_iota(jnp.int32, gate.shape, 0)
    lane = lax.broadcasted_iota(jnp.int32, (ATT_HEADS, LANES), 1)
    out = jnp.zeros((ATT_HEADS, LANES), jnp.int32)
    for r in range(MOBA_TOPK):
        m = jnp.max(gate, axis=0, keepdims=True)
        idx = jnp.min(jnp.where(gate == m, blk_id, n_blocks), axis=0, keepdims=True)
        gate = jnp.where(blk_id == idx, -jnp.inf, gate)
        out = jnp.where(lane == r, idx[0], out)
    idx_ref[0] = out


def _top_blocks(q3, kmean_parts):
    n_req = q3.shape[0]
    return pl.pallas_call(
        _top_blocks_kernel,
        grid=(n_req,),
        in_specs=[pl.BlockSpec((1, ATT_HEADS, ATT_HEAD_DIM), lambda b: (b, 0, 0))]
        + [pl.BlockSpec((1,) + part.shape[1:], lambda b: (b, 0, 0, 0)) for part in kmean_parts],
        out_specs=pl.BlockSpec((1, ATT_HEADS, LANES), lambda b: (b, 0, 0)),
        out_shape=jax.ShapeDtypeStruct((n_req, ATT_HEADS, LANES), jnp.int32),
        compiler_params=_params("parallel"),
        name="moba_top_blocks",
    )(q3, *kmean_parts)


SEL_PAGES = MOBA_TOPK * PAGES_PER_BLOCK


def _moba_sample_kernel(idx_ref, pt_ref, q_ref, kn_ref, vn_ref, ck_ref, cv_ref, o_ref, kbuf, vbuf, sem, *, n_req, n_pages):
    b = pl.program_id(0)
    slot = b % 2

    def slab_copies(req, dst_slot):
        copies = []
        for h in range(ATT_HEADS):
            for sel in range(MOBA_TOPK):
                blk = idx_ref[(req * ATT_HEADS + h) * MOBA_TOPK + sel]
                for r in range(PAGES_PER_BLOCK):
                    page = pt_ref[req * n_pages + blk * PAGES_PER_BLOCK + r]
                    j = sel * PAGES_PER_BLOCK + r
                    copies.append(pltpu.make_async_copy(ck_ref.at[page, :, h, :], kbuf.at[dst_slot, h, j], sem.at[0, dst_slot]))
                    copies.append(pltpu.make_async_copy(cv_ref.at[page, :, h, :], vbuf.at[dst_slot, h, j], sem.at[1, dst_slot]))
        return copies

    @pl.when(b == 0)
    def _():
        for cp in slab_copies(0, 0):
            cp.start()

    @pl.when(b + 1 < n_req)
    def _():
        for cp in slab_copies(b + 1, 1 - slot):
            cp.start()

    for cp in slab_copies(b, slot):
        cp.wait()

    scale = ATT_HEAD_DIM ** -0.5
    q_all = q_ref[0]
    kn_all = kn_ref[0]
    vn_all = vn_ref[0]
    for h in range(ATT_HEADS):
        q = q_all[h:h + 1, :]
        q8 = jnp.broadcast_to(q, (SUBLANES, ATT_HEAD_DIM)).astype(BF16)
        s_own = jnp.sum(q * kn_all[h:h + 1, :], axis=-1, keepdims=True) * scale
        scores = [lax.dot_general(q8, kbuf[slot, h, j].astype(BF16), NT_DIMS, preferred_element_type=F32) * scale
                  for j in range(SEL_PAGES)]
        m = s_own
        for s in scores:
            m = jnp.maximum(m, jnp.max(s, axis=-1, keepdims=True))
        p_own = jnp.exp(s_own - m)
        l = p_own
        acc = p_own * vn_all[h:h + 1, :]
        for j, s in enumerate(scores):
            p = jnp.exp(s - m)
            l = l + jnp.sum(p, axis=-1, keepdims=True)
            acc = acc + jnp.dot(p.astype(BF16), vbuf[slot, h, j].astype(BF16), preferred_element_type=F32)
        o_ref[0, h:h + 1, :] = (acc / l)[0:1, :]


def _moba_sample(q3, k3, v3, cache_k, cache_v, idx_flat, page_table_flat, n_req, n_pages):
    row_spec = pl.BlockSpec((1, ATT_HEADS, ATT_HEAD_DIM), lambda b, idx, pt: (b, 0, 0))
    any_spec = pl.BlockSpec(memory_space=pl.ANY)
    slab_buf = pltpu.VMEM((2, ATT_HEADS, SEL_PAGES, PAGE_SIZE, ATT_HEAD_DIM), F32)
    return pl.pallas_call(
        functools.partial(_moba_sample_kernel, n_req=n_req, n_pages=n_pages),
        grid_spec=pltpu.PrefetchScalarGridSpec(
            num_scalar_prefetch=2,
            grid=(n_req,),
            in_specs=[row_spec, row_spec, row_spec, any_spec, any_spec],
            out_specs=row_spec,
            scratch_shapes=[slab_buf, slab_buf, pltpu.SemaphoreType.DMA((2, 2))],
        ),
        out_shape=jax.ShapeDtypeStruct(q3.shape, F32),
        compiler_params=_params("arbitrary"),
        name="moba_sample",
    )(idx_flat, page_table_flat, q3, k3, v3, cache_k, cache_v)


def _pair_block_diag(w):
    hds, d, _ = w.shape
    wp = w.reshape(hds // 2, 2, d, d)
    z = jnp.zeros((hds // 2, d, d), w.dtype)
    top = jnp.concatenate([wp[:, 0], z], axis=-1)
    bot = jnp.concatenate([z, wp[:, 1]], axis=-1)
    return jnp.concatenate([top, bot], axis=-2)


def _lane_pad(vec, offset):
    out = jnp.zeros((1, LANES), F32)
    return out.at[0, offset:offset + vec.shape[0]].set(vec.astype(F32))


def kernel(x_prompt, x_sample, state_conv, state_rglru_h, state_gdn, cache_k, cache_v, page_table, norm_mix, norm_ffn, norm_final, w_in0, conv0_w, conv0_b, rg_wa, rg_ba, rg_wi, rg_bi, rg_lambda, gdn_a_log, gdn_dt_bias, gdn_norm, w_out0, w_qkv1, w_out1, moe_w_group, moe_b_group, moe_w_router, moe_b_router, moe_w_gate, moe_w_up, moe_w_down):
    bp, tp, d = x_prompt.shape
    bs, ts, _ = x_sample.shape
    n_pages = page_table.shape[1]
    assert ts == 1 and d == D_MODEL
    assert n_pages % PAGES_PER_BLOCK == 0
    assert n_pages // PAGES_PER_BLOCK >= MOBA_TOPK

    w_in0_b = jnp.pad(w_in0, ((0, 0), (0, IN0_PAD - IN0_DIM))).astype(BF16)
    w_out0_b = w_out0.astype(BF16)
    w_qkv1_b = w_qkv1.astype(BF16)
    w_out1_b = w_out1.astype(BF16)
    wg_b, wu_b, wd_b = moe_w_gate.astype(BF16), moe_w_up.astype(BF16), moe_w_down.astype(BF16)
    mix_w = (conv0_w, conv0_b.reshape(1, CONV_CH),
             _pair_block_diag(rg_wa).astype(BF16), rg_ba.reshape(1, RG_WIDTH),
             _pair_block_diag(rg_wi).astype(BF16), rg_bi.reshape(1, RG_WIDTH),
             rg_lambda.reshape(1, RG_WIDTH), _lane_pad(gdn_a_log, GDN_HEADS), _lane_pad(gdn_dt_bias, GDN_HEADS),
             gdn_norm.reshape(1, GDN_DV), w_out0_b)
    router_w = [jnp.pad(jnp.concatenate([moe_w_router[l], moe_w_group[l]], axis=-1),
                        ((0, 0), (0, LANES - MOE_EXPERTS - MOE_GROUPS))) for l in range(2)]
    router_b = [_lane_pad(jnp.concatenate([moe_b_router[l], moe_b_group[l]]), 0) for l in range(2)]

    xp = x_prompt.reshape(bp * tp, d)
    xs8 = jnp.pad(x_sample, ((0, 0), (0, SUBLANES - ts), (0, 0))).reshape(bs * SUBLANES, d)

    (proj_p,) = _norm_matmul(xp, norm_mix[0], w_in0_b, (IN0_PAD,), tm=512)
    (proj_s,) = _norm_matmul(xs8, norm_mix[0], w_in0_b, (IN0_PAD,), tm=bs * SUBLANES)
    pt_flat = page_table.reshape(-1)
    quarter = (n_pages // (4 * HOST_CACHE_PAGES_PER_STEP)) * HOST_CACHE_PAGES_PER_STEP
    cache_part = lambda i, span=quarter: (cache_k, pt_flat, bs, n_pages, i * quarter, span)
    xp, p_h, p_gdn, kmean_0 = _mixer(proj_p, xp,
                                     jnp.zeros((bp, SUBLANES, CONV_CH), F32), jnp.zeros((bp, 1, RG_WIDTH), F32),
                                     jnp.zeros((bp, GDN_HEADS, GDN_DK, GDN_DV), F32), mix_w,
                                     n_seq=bp, t_len=tp, tc=256, chunk=GDN_CHUNK, t_valid=tp, cache=cache_part(0))
    cbuf_s = jnp.pad(state_conv, ((0, 0), (SUBLANES - (CONV_W - 1), 0), (0, 0)))
    xs8, s_h, s_gdn = _mixer(proj_s, xs8, cbuf_s, state_rglru_h.reshape(bs, 1, RG_WIDTH), state_gdn, mix_w,
                             n_seq=bs, t_len=SUBLANES, tc=SUBLANES, chunk=SUBLANES, t_valid=ts)
    xs = xs8.reshape(bs, SUBLANES, d)[:, 0]
    p_conv = proj_p.reshape(bp, tp, IN0_PAD)[:, tp - (CONV_W - 1):, :CONV_CH]
    s_conv = jnp.concatenate([state_conv[:, ts:], proj_s.reshape(bs, SUBLANES, IN0_PAD)[:, :ts, :CONV_CH]], axis=1)

    moe0 = (norm_ffn[0], router_w[0], router_b[0], wg_b[0], wu_b[0], wd_b[0], norm_final)
    xp, kmean_1 = _sorted_moe(xp, *moe0, final_norm=False, cache=cache_part(1))
    xs = _hier_moe(xs, *moe0, tm=bs, final_norm=False)

    hd_all = ATT_HEADS * ATT_HEAD_DIM
    moe1 = (norm_ffn[1], router_w[1], router_b[1], wg_b[1], wu_b[1], wd_b[1], norm_final)
    qp, kp, vp = _norm_matmul(xp, norm_mix[1], w_qkv1_b, (hd_all,) * 3, tm=512)
    op, kmean_2 = _moba_prompt(qp, kp, vp, bp, tp, cache=cache_part(2))
    xp = _matmul_residual(op, w_out1_b, xp, tm=512)
    y_p, kmean_3 = _sorted_moe(xp, *moe1, final_norm=True, cache=cache_part(3, n_pages - 3 * quarter))

    qs, ks, vs = _norm_matmul(xs, norm_mix[1], w_qkv1_b, (hd_all,) * 3, tm=bs)
    row3 = lambda a: a.reshape(bs, ATT_HEADS, ATT_HEAD_DIM)
    idx = _top_blocks(row3(qs), (kmean_0, kmean_1, kmean_2, kmean_3))
    os_ = _moba_sample(row3(qs), row3(ks), row3(vs), cache_k, cache_v,
                       idx[:, :, :MOBA_TOPK].reshape(-1), pt_flat, bs, n_pages)
    xs = _matmul_residual(os_.reshape(bs, hd_all), w_out1_b, xs, tm=bs)
    y_s = _hier_moe(xs, *moe1, tm=bs, final_norm=True)

    heads = lambda a, b_, t_: a.reshape(b_, t_, ATT_HEADS, ATT_HEAD_DIM)
    return (y_p.reshape(bp, tp, d), y_s.reshape(bs, ts, d),
            p_conv, p_h.reshape(bp, RG_WIDTH), p_gdn,
            heads(kp, bp, tp), heads(vp, bp, tp),
            s_conv, s_h.reshape(bs, RG_WIDTH), s_gdn,
            heads(ks, bs, ts), heads(vs, bs, ts))
```

```python
import functools

import jax
import jax.numpy as jnp
from jax import lax
from jax.experimental import pallas as pl
from jax.experimental.pallas import tpu as pltpu

F32 = jnp.float32
BF16 = jnp.bfloat16
HIGHEST = lax.Precision.HIGHEST

D_MODEL = 1024
RG_WIDTH = 512
RG_HEADS = 8
RG_HEAD_DIM = 64
RG_C = 8.0
GDN_HEADS = 4
GDN_DK = 128
GDN_DV = 128
GDN_CHUNK = 64
CONV_W = 4
CONV_CH = 2048
IN0_DIM = 3080
IN0_PAD = 3200
ATT_HEADS = 8
ATT_HEAD_DIM = 128
MOBA_BLOCK = 256
MOBA_TOPK = 3
PAGE_SIZE = 128
PAGES_PER_BLOCK = MOBA_BLOCK // PAGE_SIZE
MOE_GROUPS = 4
MOE_EXPERTS_PER_GROUP = 4
MOE_EXPERTS = 16
MOE_FF = 512
RMS_EPS = 1e-6
L2_EPS = 1e-6
NEG_INF = -1e30
LOG2_E = 1.4426950408889634

LANES = 128
SUBLANES = 8
VMEM_LIMIT = 56 * 1024 * 1024
MOE_TM = 1024
NT_DIMS = (((1,), (1,)), ((), ()))
TN_DIMS = (((0,), (0,)), ((), ()))


def _params(*sem):
    return pltpu.CompilerParams(dimension_semantics=sem, vmem_limit_bytes=VMEM_LIMIT)


def _rms(x, g):
    return x * lax.rsqrt(jnp.mean(x * x, axis=-1, keepdims=True) + RMS_EPS) * g


def _softplus(x):
    return jnp.maximum(x, 0.0) + jnp.log1p(jnp.exp(-jnp.abs(x)))


def _sigmoid(x):
    return 1.0 / (1.0 + jnp.exp(-x))


def _norm_matmul_kernel(x_ref, g_ref, w_ref, *out_refs, widths):
    h = _rms(x_ref[...], g_ref[...]).astype(BF16)
    lo = 0
    for o_ref, width in zip(out_refs, widths):
        o_ref[...] = jnp.dot(h, w_ref[:, lo:lo + width], preferred_element_type=F32)
        lo += width


def _norm_matmul(x, g, w_bf16, widths, tm):
    n, d = x.shape
    assert n % tm == 0 and sum(widths) == w_bf16.shape[1]
    return pl.pallas_call(
        functools.partial(_norm_matmul_kernel, widths=widths),
        grid=(n // tm,),
        in_specs=[pl.BlockSpec((tm, d), lambda i: (i, 0)),
                  pl.BlockSpec((1, d), lambda i: (0, 0)),
                  pl.BlockSpec(w_bf16.shape, lambda i: (0, 0))],
        out_specs=[pl.BlockSpec((tm, wd), lambda i: (i, 0)) for wd in widths],
        out_shape=[jax.ShapeDtypeStruct((n, wd), F32) for wd in widths],
        compiler_params=_params("parallel"),
        name="norm_matmul",
    )(x, g.reshape(1, d), w_bf16)


def _matmul_residual_kernel(a_ref, w_ref, r_ref, o_ref):
    o_ref[...] = r_ref[...] + jnp.dot(a_ref[...].astype(BF16), w_ref[...], preferred_element_type=F32)


def _matmul_residual(a, w_bf16, res, tm):
    n, k = a.shape
    m = w_bf16.shape[1]
    assert n % tm == 0
    return pl.pallas_call(
        _matmul_residual_kernel,
        grid=(n // tm,),
        in_specs=[pl.BlockSpec((tm, k), lambda i: (i, 0)),
                  pl.BlockSpec((k, m), lambda i: (0, 0)),
                  pl.BlockSpec((tm, m), lambda i: (i, 0))],
        out_specs=pl.BlockSpec((tm, m), lambda i: (i, 0)),
        out_shape=jax.ShapeDtypeStruct((n, m), F32),
        compiler_params=_params("parallel"),
        name="matmul_residual",
    )(a, w_bf16, res)


def _linear_scan(a, u, n_rows):
    row = lax.broadcasted_iota(jnp.int32, a.shape, 0)
    d = 1
    while d < n_rows:
        a_prev = jnp.where(row >= d, pltpu.roll(a, d, 0), 1.0)
        u_prev = jnp.where(row >= d, pltpu.roll(u, d, 0), 0.0)
        u = a * u_prev + u
        a = a * a_prev
        d *= 2
    return a, u


SPLIT_POWER_MAX = 2


def _bmm(a, b):
    return jnp.einsum('bij,bjk->bik', a.astype(BF16), b.astype(BF16), preferred_element_type=F32)


def _split_bf16(x):
    hi = x.astype(BF16)
    return hi, x - hi.astype(F32)


def _bmm_split(a, b):
    a_hi, a_lo = _split_bf16(a)
    b_hi, b_lo = _split_bf16(b)
    return _bmm(a_hi, b_hi) + (_bmm(a_hi, b_lo) + _bmm(a_lo, b_hi))


def _unit_lower_solve(lmat, rhs, c):
    m = -lmat
    sol = rhs + _bmm_split(m, rhs)
    p = 2
    while p < c:
        mm = _bmm_split if p <= SPLIT_POWER_MAX else _bmm
        m = mm(m, m)
        sol = sol + mm(m, sol)
        p *= 2
    return sol


N_MIXER_INPUTS = 19


def _mixer_kernel(*refs, tc, chunk, t_valid, nt, n_cache_pages):
    if n_cache_pages:
        refs = refs[1:]
    (conv_ref, gate_ref, z_ref, ba_ref, x_ref, cbuf0_ref, h0_ref, s0_ref,
     cw_ref, cb_ref, wa_ref, rba_ref, wi_ref, rbi_ref, lam_ref, alog_ref, dtb_ref, gnorm_ref, wout_ref) = refs[:N_MIXER_INPUTS]
    page_refs = refs[N_MIXER_INPUTS:N_MIXER_INPUTS + n_cache_pages]
    x1_ref, ht_ref, st_ref = refs[N_MIXER_INPUTS + n_cache_pages:N_MIXER_INPUTS + n_cache_pages + 3]
    refs = refs[N_MIXER_INPUTS + n_cache_pages + 3:]
    if n_cache_pages:
        _block_means(page_refs, refs[0])
        refs = refs[1:]
    cbuf, hcar, state, act = refs
    t = pl.program_id(1)

    @pl.when(t == 0)
    def _():
        cbuf[0:SUBLANES, :] = cbuf0_ref[0]
        hcar[...] = h0_ref[0]
        state[...] = s0_ref[0]

    cbuf[SUBLANES:SUBLANES + tc, :] = conv_ref[...]
    base = SUBLANES - (CONV_W - 1)
    y = cb_ref[...] + cbuf[base:base + tc, :] * cw_ref[0:1, :]
    for j in range(1, CONV_W):
        y = y + cbuf[base + j:base + j + tc, :] * cw_ref[j:j + 1, :]
    cbuf[0:SUBLANES, :] = cbuf[tc:tc + SUBLANES, :]

    masked = t_valid < nt * tc
    if masked:
        valid = (t * tc + lax.broadcasted_iota(jnp.int32, (tc, 1), 0)) < t_valid

    xr = y[:, :RG_WIDTH]
    xr_b = xr.astype(BF16)
    pair = 2 * RG_HEAD_DIM
    r_lin = jnp.concatenate(
        [jnp.dot(xr_b[:, pair * i:pair * (i + 1)], wa_ref[i], preferred_element_type=F32) for i in range(RG_WIDTH // pair)], axis=-1)
    i_lin = jnp.concatenate(
        [jnp.dot(xr_b[:, pair * i:pair * (i + 1)], wi_ref[i], preferred_element_type=F32) for i in range(RG_WIDTH // pair)], axis=-1)
    r = _sigmoid(r_lin + rba_ref[...])
    gi = _sigmoid(i_lin + rbi_ref[...])
    log_a = -RG_C * r * _softplus(-lam_ref[...])
    a = jnp.exp(log_a)
    u = jnp.sqrt(-jnp.tanh(log_a) * (a * a + 1.0)) * (gi * xr)
    if masked:
        a = jnp.where(valid, a, 1.0)
        u = jnp.where(valid, u, 0.0)
    a_cum, h_loc = _linear_scan(a, u, tc)
    h = h_loc + a_cum * hcar[...]
    hcar[...] = h[tc - 1:tc, :]
    act[:, 0:RG_WIDTH] = (h * jax.nn.gelu(gate_ref[...])).astype(BF16)

    qkv = y[:, RG_WIDTH:]
    qkv = qkv * _sigmoid(qkv)
    ba = ba_ref[...]
    beta_all = _sigmoid(ba)
    g_all = -jnp.exp(alog_ref[...]) * _softplus(ba + dtb_ref[...])
    if masked:
        beta_all = jnp.where(valid, beta_all, 0.0)
        g_all = jnp.where(valid, g_all, 0.0)
    in_chunk = lax.broadcasted_iota(jnp.int32, g_all.shape, 0) & (chunk - 1)
    gcum_all = g_all
    d = 1
    while d < chunk:
        gcum_all = gcum_all + jnp.where(in_chunk >= d, pltpu.roll(gcum_all, d, 0), 0.0)
        d *= 2

    nc = tc // chunk

    def chunk_head_batch(cols_of_head):
        per_head = [cols_of_head(hd) for hd in range(GDN_HEADS)]
        return jnp.concatenate([per_head[hd][ci * chunk:(ci + 1) * chunk][None]
                                for ci in range(nc) for hd in range(GDN_HEADS)], axis=0)

    def l2_normalised(x):
        return x * lax.rsqrt(jnp.sum(x * x, axis=-1, keepdims=True) + L2_EPS)

    q_b = chunk_head_batch(lambda hd: l2_normalised(qkv[:, GDN_DK * hd:GDN_DK * (hd + 1)]) * (GDN_DK ** -0.5))
    k_b = chunk_head_batch(lambda hd: l2_normalised(qkv[:, 512 + GDN_DK * hd:512 + GDN_DK * (hd + 1)]))
    v_b = chunk_head_batch(lambda hd: qkv[:, 1024 + GDN_DV * hd:1024 + GDN_DV * (hd + 1)])
    beta_b = chunk_head_batch(lambda hd: beta_all[:, hd:hd + 1])
    gcol = chunk_head_batch(lambda hd: gcum_all[:, GDN_HEADS + hd:GDN_HEADS + hd + 1])

    nbatch = nc * GDN_HEADS
    crow = lax.broadcasted_iota(jnp.int32, (nbatch, chunk, chunk), 1)
    ccol = lax.broadcasted_iota(jnp.int32, (nbatch, chunk, chunk), 2)
    lower = crow >= ccol
    strict = crow > ccol
    grow = jnp.sum(jnp.where(crow == ccol, gcol, 0.0), axis=1, keepdims=True)
    decay = jnp.where(lower, jnp.exp(jnp.where(lower, gcol - grow, 0.0)), 0.0)
    kb = k_b * beta_b
    k_bf = k_b.astype(BF16)
    kk = jnp.einsum('bik,bjk->bij', kb.astype(BF16), k_bf, preferred_element_type=F32)
    rhs = jnp.concatenate([v_b * beta_b, kb * jnp.exp(gcol)], axis=-1)
    sol = _unit_lower_solve(jnp.where(strict, kk * decay, 0.0), rhs, chunk)
    qk = jnp.where(lower, jnp.einsum('bik,bjk->bij', q_b.astype(BF16), k_bf, preferred_element_type=F32) * decay, 0.0)
    qk_bf = qk.astype(BF16)
    q_dec = (q_b * jnp.exp(gcol)).astype(BF16)
    g_last = gcol[:, chunk - 1:chunk, :]
    k_dec = (k_b * jnp.exp(g_last - gcol)).astype(BF16)
    s_decay = jnp.exp(g_last)
    u_all = sol[:, :, :GDN_DV]
    w_bf = sol[:, :, GDN_DV:].astype(BF16)

    gnorm = gnorm_ref[...]
    z_all = z_ref[...]
    s_heads = [state[hd] for hd in range(GDN_HEADS)]
    for ci in range(nc):
        lo = ci * chunk
        for hd in range(GDN_HEADS):
            bi = ci * GDN_HEADS + hd
            s_h = s_heads[hd]
            s_bf = s_h.astype(BF16)
            v_new = u_all[bi] - jnp.dot(w_bf[bi], s_bf, preferred_element_type=F32)
            v_new_bf = v_new.astype(BF16)
            o = (jnp.dot(q_dec[bi], s_bf, preferred_element_type=F32)
                 + jnp.dot(qk_bf[bi], v_new_bf, preferred_element_type=F32))
            s_heads[hd] = s_h * s_decay[bi] + lax.dot_general(k_dec[bi], v_new_bf, TN_DIMS, preferred_element_type=F32)
            zh = z_all[lo:lo + chunk, GDN_DV * hd:GDN_DV * (hd + 1)]
            o = _rms(o, gnorm) * (zh * _sigmoid(zh))
            act[lo:lo + chunk, RG_WIDTH + GDN_DV * hd:RG_WIDTH + GDN_DV * (hd + 1)] = o.astype(BF16)
    for hd in range(GDN_HEADS):
        state[hd] = s_heads[hd]

    x1_ref[...] = x_ref[...] + jnp.dot(act[...], wout_ref[...], preferred_element_type=F32)

    @pl.when(t == nt - 1)
    def _():
        ht_ref[0] = hcar[...]
        st_ref[0] = state[...]


def _mixer(proj, x, cbuf0, h0, s0, wts, *, n_seq, t_len, tc, chunk, t_valid, cache=None):
    nt = t_len // tc
    assert t_len % tc == 0 and tc % chunk == 0 and tc % SUBLANES == 0
    row = lambda b, t: b * nt + t
    const2 = lambda b, t, *_: (0, 0)
    const3 = lambda b, t, *_: (0, 0, 0)
    (cw, cb, wa4, rba, wi4, rbi, lam, alog, dtb, gnorm, wout) = wts
    in_specs = [
        pl.BlockSpec((tc, CONV_CH), lambda b, t, *_: (row(b, t), 0)),
        pl.BlockSpec((tc, RG_WIDTH), lambda b, t, *_: (row(b, t), CONV_CH // RG_WIDTH)),
        pl.BlockSpec((tc, RG_WIDTH), lambda b, t, *_: (row(b, t), CONV_CH // RG_WIDTH + 1)),
        pl.BlockSpec((tc, LANES), lambda b, t, *_: (row(b, t), (CONV_CH + 2 * RG_WIDTH) // LANES)),
        pl.BlockSpec((tc, D_MODEL), lambda b, t, *_: (row(b, t), 0)),
        pl.BlockSpec((1, SUBLANES, CONV_CH), lambda b, t, *_: (b, 0, 0)),
        pl.BlockSpec((1, 1, RG_WIDTH), lambda b, t, *_: (b, 0, 0)),
        pl.BlockSpec((1, GDN_HEADS, GDN_DK, GDN_DV), lambda b, t, *_: (b, 0, 0, 0)),
        pl.BlockSpec(cw.shape, const2), pl.BlockSpec(cb.shape, const2),
        pl.BlockSpec(wa4.shape, const3), pl.BlockSpec(rba.shape, const2),
        pl.BlockSpec(wi4.shape, const3), pl.BlockSpec(rbi.shape, const2),
        pl.BlockSpec(lam.shape, const2), pl.BlockSpec(alog.shape, const2), pl.BlockSpec(dtb.shape, const2),
        pl.BlockSpec(gnorm.shape, const2), pl.BlockSpec(wout.shape, const2),
    ]
    out_specs = [
        pl.BlockSpec((tc, D_MODEL), lambda b, t, *_: (row(b, t), 0)),
        pl.BlockSpec((1, 1, RG_WIDTH), lambda b, t, *_: (b, 0, 0)),
        pl.BlockSpec((1, GDN_HEADS, GDN_DK, GDN_DV), lambda b, t, *_: (b, 0, 0, 0)),
    ]
    out_shape = [
        jax.ShapeDtypeStruct((n_seq * t_len, D_MODEL), F32),
        jax.ShapeDtypeStruct((n_seq, 1, RG_WIDTH), F32),
        jax.ShapeDtypeStruct((n_seq, GDN_HEADS, GDN_DK, GDN_DV), F32),
    ]
    args = [proj, proj, proj, proj, x, cbuf0, h0, s0, cw, cb, wa4, rba, wi4, rbi, lam, alog, dtb, gnorm, wout]
    assert len(args) == N_MIXER_INPUTS
    prefetch = []
    n_cache_pages = 0
    if cache is not None:
        n_cache_pages = HOST_CACHE_PAGES_PER_STEP
        prefetch = [cache[1]]
        page_specs, page_args, km_spec, km_shape = _cache_stream(cache, n_cache_pages, 2, row, n_seq * nt)
        in_specs += page_specs
        args += page_args
        out_specs.append(km_spec)
        out_shape.append(km_shape)
    return pl.pallas_call(
        functools.partial(_mixer_kernel, tc=tc, chunk=chunk, t_valid=t_valid, nt=nt, n_cache_pages=n_cache_pages),
        grid_spec=pltpu.PrefetchScalarGridSpec(
            num_scalar_prefetch=len(prefetch), grid=(n_seq, nt), in_specs=in_specs, out_specs=out_specs,
            scratch_shapes=[pltpu.VMEM((tc + SUBLANES, CONV_CH), F32),
                            pltpu.VMEM((1, RG_WIDTH), F32),
                            pltpu.VMEM((GDN_HEADS, GDN_DK, GDN_DV), F32),
                            pltpu.VMEM((tc, D_MODEL), BF16)]),
        out_shape=out_shape,
        compiler_params=_params("arbitrary", "arbitrary"),
        name="mixer0",
    )(*prefetch, *args)


def _route(h, w_ref, b_ref):
    logits = jnp.dot(h, w_ref[...], precision=HIGHEST, preferred_element_type=F32) + b_ref[...]
    lane = lax.broadcasted_iota(jnp.int32, logits.shape, 1)
    big = jnp.int32(1 << 20)
    is_grp = (lane >= MOE_EXPERTS) & (lane < MOE_EXPERTS + MOE_GROUPS)
    gl = jnp.where(is_grp, logits, -jnp.inf)
    gmax = jnp.max(gl, axis=-1, keepdims=True)
    gidx = jnp.min(jnp.where(gl == gmax, lane, big), axis=-1, keepdims=True) - MOE_EXPERTS
    g_top = 1.0 / jnp.sum(jnp.exp(gl - gmax), axis=-1, keepdims=True)
    in_grp = (lane >= gidx * MOE_EXPERTS_PER_GROUP) & (lane < (gidx + 1) * MOE_EXPERTS_PER_GROUP)
    el = jnp.where(in_grp, logits, -jnp.inf)
    m1 = jnp.max(el, axis=-1, keepdims=True)
    i1 = jnp.min(jnp.where(el == m1, lane, big), axis=-1, keepdims=True)
    el2 = jnp.where(lane == i1, -jnp.inf, el)
    m2 = jnp.max(el2, axis=-1, keepdims=True)
    i2 = jnp.min(jnp.where(el2 == m2, lane, big), axis=-1, keepdims=True)
    e2 = jnp.exp(m2 - m1)
    w1 = g_top / (1.0 + e2)
    w2 = g_top * e2 / (1.0 + e2)
    return jnp.where(lane == i1, w1, 0.0) + jnp.where(lane == i2, w2, 0.0), gidx


def _router_kernel(x_ref, g_ref, w_ref, b_ref, h_ref, gates_ref):
    h = _rms(x_ref[...], g_ref[...])
    h_ref[...] = h.astype(BF16)
    gates_ref[...], _ = _route(h, w_ref, b_ref)


def _moe_router(x, g, w_cat, b_cat, tm):
    n, d = x.shape
    return pl.pallas_call(
        _router_kernel,
        grid=(n // tm,),
        in_specs=[pl.BlockSpec((tm, d), lambda i: (i, 0)),
                  pl.BlockSpec((1, d), lambda i: (0, 0)),
                  pl.BlockSpec((d, LANES), lambda i: (0, 0)),
                  pl.BlockSpec((1, LANES), lambda i: (0, 0))],
        out_specs=[pl.BlockSpec((tm, d), lambda i: (i, 0)),
                   pl.BlockSpec((tm, LANES), lambda i: (i, 0))],
        out_shape=[jax.ShapeDtypeStruct((n, d), BF16), jax.ShapeDtypeStruct((n, LANES), F32)],
        compiler_params=_params("parallel"),
        name="moe_router",
    )(x, g.reshape(1, d), w_cat, b_cat)


def _cache_stream(cache, pages_per_step, n_grid_dims, step_of, n_steps):
    cache_k, pt_flat, n_req, n_pages, first_page, span = cache
    assert span % pages_per_step == 0 and pages_per_step % PAGES_PER_BLOCK == 0 and first_page % PAGES_PER_BLOCK == 0
    groups = span // pages_per_step
    stream_steps = n_req * groups
    assert stream_steps <= n_steps

    def group_of(ids):
        s = jnp.minimum(step_of(*ids), stream_steps - 1)
        return s // groups, s % groups

    def page_index(*args, r):
        req, grp = group_of(args[:n_grid_dims])
        return (args[-1][req * n_pages + first_page + grp * pages_per_step + r], 0, 0, 0)

    def out_index(*args):
        req, grp = group_of(args[:n_grid_dims])
        return (req, grp, 0, 0)

    in_specs = [pl.BlockSpec((1, PAGE_SIZE, ATT_HEADS, ATT_HEAD_DIM), functools.partial(page_index, r=r))
                for r in range(pages_per_step)]
    out_spec = pl.BlockSpec((1, pages_per_step // PAGES_PER_BLOCK, ATT_HEADS, ATT_HEAD_DIM), out_index)
    out_shape = jax.ShapeDtypeStruct((n_req, span // PAGES_PER_BLOCK, ATT_HEADS, ATT_HEAD_DIM), F32)
    return in_specs, [cache_k] * pages_per_step, out_spec, out_shape


def _block_means(page_refs, o_ref):
    for b in range(len(page_refs) // PAGES_PER_BLOCK):
        s = jnp.sum(page_refs[PAGES_PER_BLOCK * b][0], axis=0)
        for r in range(1, PAGES_PER_BLOCK):
            s = s + jnp.sum(page_refs[PAGES_PER_BLOCK * b + r][0], axis=0)
        o_ref[0, b] = s / MOBA_BLOCK


def _moe_expert_kernel(*refs, final_norm, n_cache_pages):
    if n_cache_pages:
        refs = refs[1:]
    h_ref, gates_ref, x_ref, wg_ref, wu_ref, wd_ref, gf_ref = refs[:7]
    page_refs = refs[7:7 + n_cache_pages]
    o_ref = refs[7 + n_cache_pages]
    e = pl.program_id(1)

    if n_cache_pages:
        _block_means(page_refs, refs[8 + n_cache_pages])

    @pl.when(e == 0)
    def _():
        o_ref[...] = x_ref[...]

    h = h_ref[...]
    a = jnp.dot(h, wg_ref[0], preferred_element_type=F32)
    b = jnp.dot(h, wu_ref[0], preferred_element_type=F32)
    gates = gates_ref[...]
    lane = lax.broadcasted_iota(jnp.int32, gates.shape, 1)
    gcol = jnp.sum(jnp.where(lane == e, gates, 0.0), axis=-1, keepdims=True)
    hid = (a * _sigmoid(a)) * b * gcol
    o_ref[...] += jnp.dot(hid.astype(BF16), wd_ref[0], preferred_element_type=F32)

    if final_norm:
        @pl.when(e == MOE_EXPERTS - 1)
        def _():
            o_ref[...] = _rms(o_ref[...], gf_ref[...])


def _moe_experts(h, gates, x, wg, wu, wd, g_final, tm, final_norm, cache=None):
    n, d = x.shape
    grid = (n // tm, MOE_EXPERTS)
    in_specs = [pl.BlockSpec((tm, d), lambda i, e, *_: (i, 0)),
                pl.BlockSpec((tm, LANES), lambda i, e, *_: (i, 0)),
                pl.BlockSpec((tm, d), lambda i, e, *_: (i, 0)),
                pl.BlockSpec((1, d, MOE_FF), lambda i, e, *_: (e, 0, 0)),
                pl.BlockSpec((1, d, MOE_FF), lambda i, e, *_: (e, 0, 0)),
                pl.BlockSpec((1, MOE_FF, d), lambda i, e, *_: (e, 0, 0)),
                pl.BlockSpec((1, d), lambda i, e, *_: (0, 0))]
    out_specs = [pl.BlockSpec((tm, d), lambda i, e, *_: (i, 0))]
    out_shape = [jax.ShapeDtypeStruct((n, d), F32)]
    args = [h, gates, x, wg, wu, wd, g_final.reshape(1, d)]
    prefetch = []
    n_cache_pages = 0
    if cache is not None:
        cache_k, pt_flat, n_req, n_pages, first_page, span = cache
        steps = grid[0] * grid[1]
        assert (n_req * span) % steps == 0 and steps % n_req == 0
        n_cache_pages = n_req * span // steps
        groups = steps // n_req
        assert n_cache_pages % PAGES_PER_BLOCK == 0 and first_page % PAGES_PER_BLOCK == 0
        prefetch = [pt_flat]

        def page_index(i, e, pt, r):
            step = i * MOE_EXPERTS + e
            return (pt[(step // groups) * n_pages + first_page + (step % groups) * n_cache_pages + r], 0, 0, 0)

        for r in range(n_cache_pages):
            in_specs.append(pl.BlockSpec((1, PAGE_SIZE, ATT_HEADS, ATT_HEAD_DIM), functools.partial(page_index, r=r)))
        args += [cache_k] * n_cache_pages
        blocks_per_step = n_cache_pages // PAGES_PER_BLOCK
        out_specs.append(pl.BlockSpec(
            (1, blocks_per_step, ATT_HEADS, ATT_HEAD_DIM),
            lambda i, e, *_: ((i * MOE_EXPERTS + e) // groups, (i * MOE_EXPERTS + e) % groups, 0, 0)))
        out_shape.append(jax.ShapeDtypeStruct((n_req, span // PAGES_PER_BLOCK, ATT_HEADS, ATT_HEAD_DIM), F32))
    outs = pl.pallas_call(
        functools.partial(_moe_expert_kernel, final_norm=final_norm, n_cache_pages=n_cache_pages),
        grid_spec=pltpu.PrefetchScalarGridSpec(
            num_scalar_prefetch=len(prefetch), grid=grid, in_specs=in_specs, out_specs=out_specs),
        out_shape=out_shape,
        compiler_params=_params("parallel", "arbitrary"),
        name="moe_experts",
    )(*prefetch, *args)
    return outs if cache is not None else outs[0]


def _hier_moe(x, g_norm, w_cat, b_cat, wg, wu, wd, g_final, tm, final_norm, cache=None):
    h, gates = _moe_router(x, g_norm, w_cat, b_cat, tm)
    return _moe_experts(h, gates, x, wg, wu, wd, g_final, tm, final_norm, cache)


SORT_TILE = 256
GRANULE = 16
SORT_ROWS = SORT_TILE + MOE_GROUPS * GRANULE
GROUP_TILE = 512
GRANULES_PER_TILE = GROUP_TILE // GRANULE
MOE_CACHE_PAGES_PER_STEP = 8
HOST_CACHE_PAGES_PER_STEP = 16


def _dispatch_kernel(x_ref, g_ref, w_ref, b_ref, hs_ref, gs_ref, slot_ref, cnt_ref):
    h = _rms(x_ref[...], g_ref[...])
    gates, gidx = _route(h, w_ref, b_ref)
    r = h.shape[0]
    lane = lax.broadcasted_iota(jnp.int32, (r, LANES), 1)
    onehot = jnp.where(lane == gidx, 1.0, 0.0)
    tok = lax.broadcasted_iota(jnp.int32, (r, r), 0)
    other = lax.broadcasted_iota(jnp.int32, (r, r), 1)
    earlier = jnp.where(other < tok, 1.0, 0.0).astype(BF16)
    prefix = jnp.dot(earlier, onehot.astype(BF16), preferred_element_type=F32)
    rank = jnp.sum(prefix * onehot, axis=-1, keepdims=True)
    counts = jnp.sum(onehot, axis=0, keepdims=True)
    padded = jnp.floor((counts + (GRANULE - 1.0)) * (1.0 / GRANULE)) * GRANULE
    lane1 = lax.broadcasted_iota(jnp.int32, (1, LANES), 1)
    seg_start = jnp.zeros((1, LANES), F32)
    running = jnp.zeros((1, 1), F32)
    for grp in range(MOE_GROUPS):
        seg_start = jnp.where(lane1 == grp, running, seg_start)
        running = running + padded[:, grp:grp + 1]
    slot = jnp.sum(onehot * seg_start, axis=-1, keepdims=True) + rank
    slot_b = jnp.broadcast_to(slot, (r, LANES))
    slot_row = slot_b.T[0:1, :]
    srow = lax.broadcasted_iota(jnp.int32, (SORT_ROWS, r), 0).astype(F32)
    perm = jnp.where(srow == slot_row, 1.0, 0.0)
    hs_ref[...] = jnp.dot(perm.astype(BF16), h.astype(BF16), preferred_element_type=F32).astype(BF16)
    gs_ref[...] = jnp.dot(perm, gates, precision=HIGHEST, preferred_element_type=F32)
    slot_ref[...] = slot_b.astype(jnp.int32)
    cnt_ref[0] = jnp.broadcast_to(counts, (SUBLANES, LANES)).astype(jnp.int32)


def _moe_dispatch(x, g, w_cat, b_cat):
    n, d = x.shape
    nt = n // SORT_TILE
    return pl.pallas_call(
        _dispatch_kernel,
        grid=(nt,),
        in_specs=[pl.BlockSpec((SORT_TILE, d), lambda i: (i, 0)),
                  pl.BlockSpec((1, d), lambda i: (0, 0)),
                  pl.BlockSpec((d, LANES), lambda i: (0, 0)),
                  pl.BlockSpec((1, LANES), lambda i: (0, 0))],
        out_specs=[pl.BlockSpec((SORT_ROWS, d), lambda i: (i, 0)),
                   pl.BlockSpec((SORT_ROWS, LANES), lambda i: (i, 0)),
                   pl.BlockSpec((SORT_TILE, LANES), lambda i: (i, 0)),
                   pl.BlockSpec((1, SUBLANES, LANES), lambda i: (i, 0, 0))],
        out_shape=[jax.ShapeDtypeStruct((nt * SORT_ROWS, d), BF16),
                   jax.ShapeDtypeStruct((nt * SORT_ROWS, LANES), F32),
                   jax.ShapeDtypeStruct((n, LANES), jnp.int32),
                   jax.ShapeDtypeStruct((nt, SUBLANES, LANES), jnp.int32)],
        compiler_params=_params("parallel"),
        name="moe_dispatch",
    )(x, g.reshape(1, d), w_cat, b_cat)


def _group_tile_tables(cnt, n_group_tiles):
    nt, ng = cnt.shape
    pc = (cnt + GRANULE - 1) // GRANULE * GRANULE
    seg_row = (jnp.cumsum(pc, axis=1) - pc).T.reshape(-1)
    seg_tile = jnp.tile(jnp.arange(nt), ng)
    ngr = (pc // GRANULE).T.reshape(-1)
    seg_end = jnp.cumsum(ngr)
    seg_first = seg_end - ngr
    k = jnp.arange(nt * SORT_ROWS // GRANULE)[:, None]
    in_seg = (k >= seg_first[None, :]) & (k < seg_end[None, :])
    seg_base = seg_tile * SORT_ROWS + seg_row - seg_first * GRANULE
    gran_row = jnp.sum(jnp.where(in_seg, seg_base[None, :] + k * GRANULE, 0), axis=1)
    group_gran = jnp.sum(ngr.reshape(ng, nt), axis=1)
    group_first = jnp.cumsum(group_gran) - group_gran
    tiles = (group_gran + GRANULES_PER_TILE - 1) // GRANULES_PER_TILE
    tile_end = jnp.cumsum(tiles)
    tile_first = tile_end - tiles
    j = jnp.arange(n_group_tiles)[:, None]
    in_group = (j >= tile_first[None, :]) & (j < tile_end[None, :])
    pick = lambda per_group: jnp.sum(jnp.where(in_group, per_group[None, :], 0), axis=1)
    used = jnp.any(in_group, axis=1)
    tg = jnp.where(used, pick(jnp.arange(ng)), ng - 1)
    local = j[:, 0] - pick(tile_first)
    first = pick(group_first) + local * GRANULES_PER_TILE
    count = jnp.where(used, jnp.clip(pick(group_gran) - local * GRANULES_PER_TILE, 0, GRANULES_PER_TILE), 0)
    i32 = lambda a: a.astype(jnp.int32)
    return i32(gran_row), i32(tg), i32(jnp.where(used, first, 0)), i32(count)


def _moe_group_kernel(*refs, n_cache_pages):
    gran_ref, tgrp_ref, tfirst_ref, tcount_ref = refs[:4]
    refs = refs[5:] if n_cache_pages else refs[4:]
    wg_ref, wu_ref, wd_ref, hs_ref, gs_ref, _ = refs[:6]
    page_refs = refs[6:6 + n_cache_pages]
    y_ref = refs[6 + n_cache_pages]
    refs = refs[7 + n_cache_pages:]
    if n_cache_pages:
        _block_means(page_refs, refs[0])
        refs = refs[1:]
    hbuf, gbuf, ybuf, sem = refs
    j = pl.program_id(0)
    e = pl.program_id(1)
    count = tcount_ref[j]
    first = tfirst_ref[j]

    def buf_rows(i):
        return pl.ds(pl.multiple_of(i * GRANULE, GRANULE), GRANULE)

    def sorted_rows(i):
        return pl.ds(pl.multiple_of(gran_ref[first + i], GRANULE), GRANULE)

    def gather_copies(i):
        return (pltpu.make_async_copy(hs_ref.at[sorted_rows(i), :], hbuf.at[buf_rows(i), :], sem.at[0]),
                pltpu.make_async_copy(gs_ref.at[sorted_rows(i), :], gbuf.at[buf_rows(i), :], sem.at[1]))

    def scatter_copy(i):
        return pltpu.make_async_copy(ybuf.at[buf_rows(i), :], y_ref.at[sorted_rows(i), :], sem.at[2])

    def for_granules(lo, hi, fn):
        def body(i, carry):
            fn(i)
            return carry
        lax.fori_loop(lo, hi, body, 0)

    @pl.when(jnp.logical_and(e == 0, count > 0))
    def _():
        def start(i):
            for priority, cp in enumerate(gather_copies(i)):
                cp.start(priority=priority)

        def wait(i):
            for cp in gather_copies(i):
                cp.wait()

        def clear(i):
            hbuf[buf_rows(i), :] = jnp.zeros((GRANULE, hbuf.shape[1]), hbuf.dtype)
            gbuf[buf_rows(i), :] = jnp.zeros((GRANULE, gbuf.shape[1]), gbuf.dtype)

        for_granules(0, count, start)
        for_granules(0, count, wait)
        for_granules(count, GRANULES_PER_TILE, clear)
        ybuf[...] = jnp.zeros_like(ybuf)

    @pl.when(count > 0)
    def _():
        h = hbuf[...]
        a = jnp.dot(h, wg_ref[0], preferred_element_type=F32)
        b = jnp.dot(h, wu_ref[0], preferred_element_type=F32)
        gates = gbuf[...]
        lane = lax.broadcasted_iota(jnp.int32, gates.shape, 1)
        expert = tgrp_ref[j] * MOE_EXPERTS_PER_GROUP + e
        gcol = jnp.sum(jnp.where(lane == expert, gates, 0.0), axis=-1, keepdims=True)
        hid = (a * _sigmoid(a)) * b * gcol
        ybuf[...] += jnp.dot(hid.astype(BF16), wd_ref[0], preferred_element_type=F32)

    @pl.when(jnp.logical_and(e == MOE_EXPERTS_PER_GROUP - 1, count > 0))
    def _():
        def start_pair(p):
            scatter_copy(2 * p).start(priority=0)

            @pl.when(2 * p + 1 < count)
            def _():
                scatter_copy(2 * p + 1).start(priority=1)

        for_granules(0, (count + 1) // 2, start_pair)
        for_granules(0, count, lambda i: scatter_copy(i).wait())


def _moe_group_experts(tables, hs, gs, wg, wu, wd, n_group_tiles, cache):
    rows, d = hs.shape
    grid = (n_group_tiles, MOE_EXPERTS_PER_GROUP)

    def weight_index(j, e, gran, tgrp, tfirst, tcount, *_):
        return (tgrp[j] * MOE_EXPERTS_PER_GROUP + jnp.where(tcount[j] > 0, e, MOE_EXPERTS_PER_GROUP - 1), 0, 0)

    any_spec = pl.BlockSpec(memory_space=pl.ANY)
    in_specs = [pl.BlockSpec((1, d, MOE_FF), weight_index),
                pl.BlockSpec((1, d, MOE_FF), weight_index),
                pl.BlockSpec((1, MOE_FF, d), weight_index),
                any_spec, any_spec, any_spec]
    prefetch = list(tables)
    args = [wg, wu, wd, hs, gs, jnp.zeros((rows, d), F32)]
    out_specs = [any_spec]
    out_shape = [jax.ShapeDtypeStruct((rows, d), F32)]
    n_cache_pages = 0
    if cache is not None:
        n_cache_pages = MOE_CACHE_PAGES_PER_STEP
        prefetch.append(cache[1])
        page_specs, page_args, km_spec, km_shape = _cache_stream(
            cache, n_cache_pages, 2, lambda j, e: j * MOE_EXPERTS_PER_GROUP + e, grid[0] * grid[1])
        in_specs += page_specs
        args += page_args
        out_specs.append(km_spec)
        out_shape.append(km_shape)
    zero_arg = len(prefetch) + 5
    outs = pl.pallas_call(
        functools.partial(_moe_group_kernel, n_cache_pages=n_cache_pages),
        grid_spec=pltpu.PrefetchScalarGridSpec(
            num_scalar_prefetch=len(prefetch), grid=grid, in_specs=in_specs, out_specs=out_specs,
            scratch_shapes=[pltpu.VMEM((GROUP_TILE, d), BF16), pltpu.VMEM((GROUP_TILE, LANES), F32),
                            pltpu.VMEM((GROUP_TILE, d), F32), pltpu.SemaphoreType.DMA((3,))]),
        out_shape=out_shape,
        input_output_aliases={zero_arg: 0},
        compiler_params=_params("arbitrary", "arbitrary"),
        name="moe_group_experts",
    )(*prefetch, *args)
    return outs if cache is not None else outs[0]


def _combine_kernel(x_ref, y_ref, slot_ref, gf_ref, o_ref, *, final_norm):
    slot = slot_ref[:, 0:1]
    col = lax.broadcasted_iota(jnp.int32, (x_ref.shape[0], SORT_ROWS), 1)
    unperm = jnp.where(col == slot, 1.0, 0.0)
    out = x_ref[...] + jnp.dot(unperm, y_ref[...], precision=HIGHEST, preferred_element_type=F32)
    if final_norm:
        out = _rms(out, gf_ref[...])
    o_ref[...] = out


def _moe_combine(x, y_sorted, slot, g_final, final_norm):
    n, d = x.shape
    return pl.pallas_call(
        functools.partial(_combine_kernel, final_norm=final_norm),
        grid=(n // SORT_TILE,),
        in_specs=[pl.BlockSpec((SORT_TILE, d), lambda i: (i, 0)),
                  pl.BlockSpec((SORT_ROWS, d), lambda i: (i, 0)),
                  pl.BlockSpec((SORT_TILE, LANES), lambda i: (i, 0)),
                  pl.BlockSpec((1, d), lambda i: (0, 0))],
        out_specs=pl.BlockSpec((SORT_TILE, d), lambda i: (i, 0)),
        out_shape=jax.ShapeDtypeStruct((n, d), F32),
        compiler_params=_params("parallel"),
        name="moe_combine",
    )(x, y_sorted, slot, g_final.reshape(1, d))


def _sorted_moe(x, g_norm, w_cat, b_cat, wg, wu, wd, g_final, final_norm, cache=None):
    n = x.shape[0]
    assert n % SORT_TILE == 0
    hs, gs, slot, cnt = _moe_dispatch(x, g_norm, w_cat, b_cat)
    n_granules = hs.shape[0] // GRANULE
    n_group_tiles = -(-n_granules // GRANULES_PER_TILE) + MOE_GROUPS
    tables = _group_tile_tables(cnt[:, 0, :MOE_GROUPS], n_group_tiles)
    outs = _moe_group_experts(tables, hs, gs, wg, wu, wd, n_group_tiles, cache)
    y_sorted = outs[0] if cache is not None else outs
    out = _moe_combine(x, y_sorted, slot, g_final, final_norm)
    return (out, outs[1]) if cache is not None else out


def _moba_prompt_kernel(*refs, t_len, n_cache_pages):
    if n_cache_pages:
        _block_means(refs[4:4 + n_cache_pages], refs[5 + n_cache_pages])
        refs = refs[1:4] + refs[4 + n_cache_pages:5 + n_cache_pages]
    q_ref, k_ref, v_ref, o_ref = refs
    nb = t_len // MOBA_BLOCK
    blk = MOBA_BLOCK
    c = (ATT_HEAD_DIM ** -0.5) * LOG2_E
    kf = k_ref[...]
    k_bf = kf.astype(BF16)
    q_t = q_ref[...].T
    q_t_bf = (q_t * c).astype(BF16)
    v_t_bf = v_ref[...].T.astype(BF16)
    kmean = jnp.mean(kf.reshape(nb, blk, ATT_HEAD_DIM), axis=1)
    key_pos =lax.broadcasted_iota(jnp.int32, (blk, blk), 0)
    query_pos = lax.broadcasted_iota(jnp.int32, (blk, blk), 1)
    causal = key_pos <= query_pos
    for i in range(nb):
        qs = slice(i * blk, (i + 1) * blk)
        sel = [None] * i
        if i > MOBA_TOPK:
            gate = jnp.dot(kmean, q_t[:, qs], precision=HIGHEST, preferred_element_type=F32)
            g = [gate[j:j + 1, :] for j in range(i)]
            rank = [jnp.zeros((1, blk), F32) for _ in range(i)]
            for lo_j in range(i):
                for hi_j in range(lo_j + 1, i):
                    lo_wins = jnp.where(g[lo_j] >= g[hi_j], 1.0, 0.0)
                    rank[hi_j] = rank[hi_j] + lo_wins
                    rank[lo_j] = rank[lo_j] + (1.0 - lo_wins)
            sel = [r < float(MOBA_TOPK) for r in rank]
        pieces = []
        s_max = None
        for j in range(i + 1):
            s = jnp.dot(k_bf[j * blk:(j + 1) * blk], q_t_bf[:, qs], preferred_element_type=F32)
            if j == i:
                s = jnp.where(causal, s, NEG_INF)
            elif sel[j] is not None:
                s = jnp.where(sel[j], s, NEG_INF)
            pieces.append(s)
            s_max = s if s_max is None else jnp.maximum(s_max, s)
        m = jnp.max(s_max, axis=0, keepdims=True)
        p_sum = jnp.zeros((blk, blk), F32)
        acc = jnp.zeros((ATT_HEAD_DIM, blk), F32)
        for j, s in enumerate(pieces):
            p = jnp.exp2(s - m)
            p_sum = p_sum + p
            acc = acc + jnp.dot(v_t_bf[:, j * blk:(j + 1) * blk], p.astype(BF16), preferred_element_type=F32)
        o_ref[qs, :] = (acc / jnp.sum(p_sum, axis=0, keepdims=True)).T


def _moba_prompt(q, k, v, n_seq, t_len, cache=None):
    assert t_len % MOBA_BLOCK == 0
    spec = pl.BlockSpec((t_len, ATT_HEAD_DIM), lambda b, h, *_: (b, h))
    in_specs, args, out_specs, out_shape = [spec, spec, spec], [q, k, v], [spec], [jax.ShapeDtypeStruct(q.shape, F32)]
    prefetch = []
    n_cache_pages = 0
    if cache is not None:
        n_cache_pages = HOST_CACHE_PAGES_PER_STEP
        prefetch = [cache[1]]
        page_specs, page_args, km_spec, km_shape = _cache_stream(
            cache, n_cache_pages, 2, lambda b, h: b * ATT_HEADS + h, n_seq * ATT_HEADS)
        in_specs += page_specs
        args += page_args
        out_specs.append(km_spec)
        out_shape.append(km_shape)
    outs = pl.pallas_call(
        functools.partial(_moba_prompt_kernel, t_len=t_len, n_cache_pages=n_cache_pages),
        grid_spec=pltpu.PrefetchScalarGridSpec(
            num_scalar_prefetch=len(prefetch), grid=(n_seq, ATT_HEADS), in_specs=in_specs, out_specs=out_specs),
        out_shape=out_shape,
        compiler_params=_params("parallel", "parallel"),
        name="moba_prompt",
    )(*prefetch, *args)
    return outs if cache is not None else outs[0]


def _top_blocks_kernel(q_ref, *refs):
    kmean_refs, idx_ref = refs[:-1], refs[-1]
    q = q_ref[0]
    kmean = jnp.concatenate([r[0] for r in kmean_refs], axis=0)
    gate = jnp.sum(kmean * q[None, :, :], axis=-1, keepdims=True)
    n_blocks = gate.shape[0]
    blk_id = lax.broadcasted_iota(jnp.int32, gate.shape, 0)
    lane = lax.broadcasted_iota(jnp.int32, (ATT_HEADS, LANES), 1)
    out = jnp.zeros((ATT_HEADS, LANES), jnp.int32)
    for r in range(MOBA_TOPK):
        m = jnp.max(gate, axis=0, keepdims=True)
        idx = jnp.min(jnp.where(gate == m, blk_id, n_blocks), axis=0, keepdims=True)
        gate = jnp.where(blk_id == idx, -jnp.inf, gate)
        out = jnp.where(lane == r, idx[0], out)
    idx_ref[0] = out


def _top_blocks(q3, kmean_parts):
    n_req = q3.shape[0]
    return pl.pallas_call(
        _top_blocks_kernel,
        grid=(n_req,),
        in_specs=[pl.BlockSpec((1, ATT_HEADS, ATT_HEAD_DIM), lambda b: (b, 0, 0))]
        + [pl.BlockSpec((1,) + part.shape[1:], lambda b: (b, 0, 0, 0)) for part in kmean_parts],
        out_specs=pl.BlockSpec((1, ATT_HEADS, LANES), lambda b: (b, 0, 0)),
        out_shape=jax.ShapeDtypeStruct((n_req, ATT_HEADS, LANES), jnp.int32),
        compiler_params=_params("parallel"),
        name="moba_top_blocks",
    )(q3, *kmean_parts)


SEL_PAGES = MOBA_TOPK * PAGES_PER_BLOCK


def _moba_sample_kernel(idx_ref, pt_ref, q_ref, kn_ref, vn_ref, ck_ref, cv_ref, o_ref, kbuf, vbuf, sem, *, n_req, n_pages):
    b = pl.program_id(0)
    slot = b % 2

    def slab_copies(req, dst_slot):
        copies = []
        for h in range(ATT_HEADS):
            for sel in range(MOBA_TOPK):
                blk = idx_ref[(req * ATT_HEADS + h) * MOBA_TOPK + sel]
                for r in range(PAGES_PER_BLOCK):
                    page = pt_ref[req * n_pages + blk * PAGES_PER_BLOCK + r]
                    j = sel * PAGES_PER_BLOCK + r
                    copies.append(pltpu.make_async_copy(ck_ref.at[page, :, h, :], kbuf.at[dst_slot, h, j], sem.at[0, dst_slot]))
                    copies.append(pltpu.make_async_copy(cv_ref.at[page, :, h, :], vbuf.at[dst_slot, h, j], sem.at[1, dst_slot]))
        return copies

    @pl.when(b == 0)
    def _():
        for n, cp in enumerate(slab_copies(0, 0)):
            cp.start(priority=n % 2)

    @pl.when(b + 1 < n_req)
    def _():
        for n, cp in enumerate(slab_copies(b + 1, 1 - slot)):
            cp.start(priority=n % 2)

    for cp in slab_copies(b, slot):
        cp.wait()

    scale = ATT_HEAD_DIM ** -0.5
    q_all = q_ref[0]
    kn_all = kn_ref[0]
    vn_all = vn_ref[0]
    for h in range(ATT_HEADS):
        q = q_all[h:h + 1, :]
        q8 = jnp.broadcast_to(q, (SUBLANES, ATT_HEAD_DIM)).astype(BF16)
        s_own = jnp.sum(q * kn_all[h:h + 1, :], axis=-1, keepdims=True) * scale
        scores = [lax.dot_general(q8, kbuf[slot, h, j].astype(BF16), NT_DIMS, preferred_element_type=F32) * scale
                  for j in range(SEL_PAGES)]
        m = s_own
        for s in scores:
            m = jnp.maximum(m, jnp.max(s, axis=-1, keepdims=True))
        p_own = jnp.exp(s_own - m)
        l = p_own
        acc = p_own * vn_all[h:h + 1, :]
        for j, s in enumerate(scores):
            p = jnp.exp(s - m)
            l = l + jnp.sum(p, axis=-1, keepdims=True)
            acc = acc + jnp.dot(p.astype(BF16), vbuf[slot, h, j].astype(BF16), preferred_element_type=F32)
        o_ref[0, h:h + 1, :] = (acc / l)[0:1, :]


def _moba_sample(q3, k3, v3, cache_k, cache_v, idx_flat, page_table_flat, n_req, n_pages):
    row_spec = pl.BlockSpec((1, ATT_HEADS, ATT_HEAD_DIM), lambda b, idx, pt: (b, 0, 0))
    any_spec = pl.BlockSpec(memory_space=pl.ANY)
    slab_buf = pltpu.VMEM((2, ATT_HEADS, SEL_PAGES, PAGE_SIZE, ATT_HEAD_DIM), F32)
    return pl.pallas_call(
        functools.partial(_moba_sample_kernel, n_req=n_req, n_pages=n_pages),
        grid_spec=pltpu.PrefetchScalarGridSpec(
            num_scalar_prefetch=2,
            grid=(n_req,),
            in_specs=[row_spec, row_spec, row_spec, any_spec, any_spec],
            out_specs=row_spec,
            scratch_shapes=[slab_buf, slab_buf, pltpu.SemaphoreType.DMA((2, 2))],
        ),
        out_shape=jax.ShapeDtypeStruct(q3.shape, F32),
        compiler_params=_params("arbitrary"),
        name="moba_sample",
    )(idx_flat, page_table_flat, q3, k3, v3, cache_k, cache_v)


def _pair_block_diag(w):
    hds, d, _ = w.shape
    wp = w.reshape(hds // 2, 2, d, d)
    z = jnp.zeros((hds // 2, d, d), w.dtype)
    top = jnp.concatenate([wp[:, 0], z], axis=-1)
    bot = jnp.concatenate([z, wp[:, 1]], axis=-1)
    return jnp.concatenate([top, bot], axis=-2)


def _lane_pad(vec, offset):
    out = jnp.zeros((1, LANES), F32)
    return out.at[0, offset:offset + vec.shape[0]].set(vec.astype(F32))


def kernel(x_prompt, x_sample, state_conv, state_rglru_h, state_gdn, cache_k, cache_v, page_table, norm_mix, norm_ffn, norm_final, w_in0, conv0_w, conv0_b, rg_wa, rg_ba, rg_wi, rg_bi, rg_lambda, gdn_a_log, gdn_dt_bias, gdn_norm, w_out0, w_qkv1, w_out1, moe_w_group, moe_b_group, moe_w_router, moe_b_router, moe_w_gate, moe_w_up, moe_w_down):
    bp, tp, d = x_prompt.shape
    bs, ts, _ = x_sample.shape
    n_pages = page_table.shape[1]
    assert ts == 1 and d == D_MODEL
    assert n_pages % PAGES_PER_BLOCK == 0
    assert n_pages // PAGES_PER_BLOCK >= MOBA_TOPK

    w_in0_b = jnp.pad(w_in0, ((0, 0), (0, IN0_PAD - IN0_DIM))).astype(BF16)
    w_out0_b = w_out0.astype(BF16)
    w_qkv1_b = w_qkv1.astype(BF16)
    w_out1_b = w_out1.astype(BF16)
    wg_b, wu_b, wd_b = moe_w_gate.astype(BF16), moe_w_up.astype(BF16), moe_w_down.astype(BF16)
    mix_w = (conv0_w, conv0_b.reshape(1, CONV_CH),
             _pair_block_diag(rg_wa).astype(BF16), rg_ba.reshape(1, RG_WIDTH),
             _pair_block_diag(rg_wi).astype(BF16), rg_bi.reshape(1, RG_WIDTH),
             rg_lambda.reshape(1, RG_WIDTH), _lane_pad(gdn_a_log, GDN_HEADS), _lane_pad(gdn_dt_bias, GDN_HEADS),
             gdn_norm.reshape(1, GDN_DV), w_out0_b)
    router_w = [jnp.pad(jnp.concatenate([moe_w_router[l], moe_w_group[l]], axis=-1),
                        ((0, 0), (0, LANES - MOE_EXPERTS - MOE_GROUPS))) for l in range(2)]
    router_b = [_lane_pad(jnp.concatenate([moe_b_router[l], moe_b_group[l]]), 0) for l in range(2)]

    xp = x_prompt.reshape(bp * tp, d)
    xs8 = jnp.pad(x_sample, ((0, 0), (0, SUBLANES - ts), (0, 0))).reshape(bs * SUBLANES, d)

    (proj_p,) = _norm_matmul(xp, norm_mix[0], w_in0_b, (IN0_PAD,), tm=512)
    (proj_s,) = _norm_matmul(xs8, norm_mix[0], w_in0_b, (IN0_PAD,), tm=bs * SUBLANES)
    pt_flat = page_table.reshape(-1)
    quarter = (n_pages // (4 * HOST_CACHE_PAGES_PER_STEP)) * HOST_CACHE_PAGES_PER_STEP
    cache_part = lambda i, span=quarter: (cache_k, pt_flat, bs, n_pages, i * quarter, span)
    xp, p_h, p_gdn, kmean_0 = _mixer(proj_p, xp,
                                     jnp.zeros((bp, SUBLANES, CONV_CH), F32), jnp.zeros((bp, 1, RG_WIDTH), F32),
                                     jnp.zeros((bp, GDN_HEADS, GDN_DK, GDN_DV), F32), mix_w,
                                     n_seq=bp, t_len=tp, tc=256, chunk=GDN_CHUNK, t_valid=tp, cache=cache_part(0))
    cbuf_s = jnp.pad(state_conv, ((0, 0), (SUBLANES - (CONV_W - 1), 0), (0, 0)))
    xs8, s_h, s_gdn = _mixer(proj_s, xs8, cbuf_s, state_rglru_h.reshape(bs, 1, RG_WIDTH), state_gdn, mix_w,
                             n_seq=bs, t_len=SUBLANES, tc=SUBLANES, chunk=SUBLANES, t_valid=ts)
    xs = xs8.reshape(bs, SUBLANES, d)[:, 0]
    p_conv = proj_p.reshape(bp, tp, IN0_PAD)[:, tp - (CONV_W - 1):, :CONV_CH]
    s_conv = jnp.concatenate([state_conv[:, ts:], proj_s.reshape(bs, SUBLANES, IN0_PAD)[:, :ts, :CONV_CH]], axis=1)

    moe0 = (norm_ffn[0], router_w[0], router_b[0], wg_b[0], wu_b[0], wd_b[0], norm_final)
    xp, kmean_1 = _sorted_moe(xp, *moe0, final_norm=False, cache=cache_part(1))
    xs = _hier_moe(xs, *moe0, tm=bs, final_norm=False)

    hd_all = ATT_HEADS * ATT_HEAD_DIM
    moe1 = (norm_ffn[1], router_w[1], router_b[1], wg_b[1], wu_b[1], wd_b[1], norm_final)
    qp, kp, vp = _norm_matmul(xp, norm_mix[1], w_qkv1_b, (hd_all,) * 3, tm=512)
    op, kmean_2 = _moba_prompt(qp, kp, vp, bp, tp, cache=cache_part(2))
    xp = _matmul_residual(op, w_out1_b, xp, tm=512)
    y_p, kmean_3 = _sorted_moe(xp, *moe1, final_norm=True, cache=cache_part(3, n_pages - 3 * quarter))

    qs, ks, vs = _norm_matmul(xs, norm_mix[1], w_qkv1_b, (hd_all,) * 3, tm=bs)
    row3 = lambda a: a.reshape(bs, ATT_HEADS, ATT_HEAD_DIM)
    idx = _top_blocks(row3(qs), (kmean_0, kmean_1, kmean_2, kmean_3))
    os_ = _moba_sample(row3(qs), row3(ks), row3(vs), cache_k, cache_v,
                       idx[:, :, :MOBA_TOPK].reshape(-1), pt_flat, bs, n_pages)
    xs = _matmul_residual(os_.reshape(bs, hd_all), w_out1_b, xs, tm=bs)
    y_s = _hier_moe(xs, *moe1, tm=bs, final_norm=True)

    heads = lambda a, b_, t_: a.reshape(b_, t_, ATT_HEADS, ATT_HEAD_DIM)
    return (y_p.reshape(bp, tp, d), y_s.reshape(bs, ts, d),
            p_conv, p_h.reshape(bp, RG_WIDTH), p_gdn,
            heads(kp, bp, tp), heads(vp, bp, tp),
            s_conv, s_h.reshape(bs, RG_WIDTH), s_gdn,
            heads(ks, bs, ts), heads(vs, bs, ts))
```
